```python
import math
import jax, jax.numpy as jnp
from jax import lax
import numpy as np

D_MODEL = 1024
BATCH = 8
SEQ = 2048
DEPTH = 4
DEC_BATCH = 32
DEC_SEQ = 8
PAST_LEN = 16384
PAGE_SIZE = 128

N_A_LAYERS = DEPTH // 2
N_B_LAYERS = DEPTH - N_A_LAYERS
FFN_DIM = 2816
GDN_HEADS = 8
GDN_DK = 128
GDN_DV = 128
GDN_QK_DIM = GDN_HEADS * GDN_DK
GDN_V_DIM = GDN_HEADS * GDN_DV
GDN_CONV_DIM = 2 * GDN_QK_DIM + GDN_V_DIM
GDN_IN_DIM = GDN_CONV_DIM + GDN_V_DIM + 2 * GDN_HEADS
CONV_W = 4
GDN_CHUNK = 64
MLA_HEADS = 8
Q_LORA = 512
KV_LORA = 256
NOPE_DIM = 128
ROPE_DIM = 64
V_HEAD_DIM = 128
QK_HEAD_DIM = NOPE_DIM + ROPE_DIM
ROPE_THETA = 10000.0
Q_BLOCK = 128
EPS = 1e-6

kernel_name = "yoco_gdn_mla_macaron_decode_step"


def rmsnorm(x, g):
    xf = x.astype(jnp.float32)
    y = xf * lax.rsqrt(jnp.mean(xf * xf, axis=-1, keepdims=True) + EPS)
    return (y * g.astype(jnp.float32)).astype(x.dtype)


def l2norm(x):
    xf = x.astype(jnp.float32)
    return xf * lax.rsqrt(jnp.sum(xf * xf, axis=-1, keepdims=True) + EPS)


def swiglu(h, w_in, w_out):
    gu = h @ w_in
    return (jax.nn.silu(gu[..., :FFN_DIM]) * gu[..., FFN_DIM:]) @ w_out


def rope_tables(pos):
    half = ROPE_DIM // 2
    inv_freq = ROPE_THETA ** (-jnp.arange(half, dtype=jnp.float32) / half)
    ang = pos.astype(jnp.float32)[:, None] * inv_freq[None, :]
    return jnp.cos(ang), jnp.sin(ang)


def apply_rope(x, cos, sin):
    half = ROPE_DIM // 2
    shp = (cos.shape[0],) + (1,) * (x.ndim - 3) + (half,)
    c, s = cos.reshape(shp), sin.reshape(shp)
    xf = x.astype(jnp.float32)
    x1, x2 = xf[..., :half], xf[..., half:]
    return jnp.concatenate([x1 * c - x2 * s, x2 * c + x1 * s], axis=-1).astype(x.dtype)


def causal_conv(x, buf, w):
    L = x.shape[1]
    xp = jnp.concatenate([buf.astype(x.dtype), x], axis=1)
    out = xp[:, 0:L] * w[0]
    for j in range(1, CONV_W):
        out = out + xp[:, j:j + L] * w[j]
    return out, xp[:, -(CONV_W - 1):]


def gated_delta_chunked(q, k, v, g, beta, S0):
    B, L, H, DK = q.shape
    DV = v.shape[-1]
    C = math.gcd(L, GDN_CHUNK)
    N = L // C
    f32 = jnp.float32
    q = q.astype(f32) * (DK ** -0.5)
    k, v, g, beta = k.astype(f32), v.astype(f32), g.astype(f32), beta.astype(f32)

    def chunks(t):
        return t.reshape((B, N, C) + t.shape[2:]).swapaxes(2, 3)

    qc, kc, vc, gc, bc = chunks(q), chunks(k), chunks(v), chunks(g), chunks(beta)
    gcum = jnp.cumsum(gc, axis=-1)
    tril = jnp.tril(jnp.ones((C, C), dtype=bool))
    strict = jnp.tril(jnp.ones((C, C), dtype=bool), -1)
    diff = gcum[..., :, None] - gcum[..., None, :]
    decay = jnp.where(tril, jnp.exp(jnp.where(tril, diff, 0.0)), 0.0)
    kb = kc * bc[..., None]
    lmat = jnp.where(strict, jnp.einsum('bnhid,bnhjd->bnhij', kb, kc) * decay, 0.0)
    a_mat = jnp.eye(C, dtype=f32) + lmat
    rhs = jnp.concatenate([vc * bc[..., None], kb * jnp.exp(gcum)[..., None]], axis=-1)
    sol = lax.linalg.triangular_solve(a_mat, rhs, left_side=True, lower=True, unit_diagonal=True)
    u, w = sol[..., :DV], sol[..., DV:]
    qk = jnp.einsum('bnhid,bnhjd->bnhij', qc, kc) * decay
    qg = qc * jnp.exp(gcum)[..., None]
    glast = gcum[..., -1]
    kdec = kc * jnp.exp(glast[..., None] - gcum)[..., None]

    def step(S, inp):
        u_i, w_i, qk_i, qg_i, kdec_i, gl_i = inp
        v_new = u_i - jnp.einsum('bhck,bhkv->bhcv', w_i, S)
        o = jnp.einsum('bhck,bhkv->bhcv', qg_i, S) + jnp.einsum('bhij,bhjv->bhiv', qk_i, v_new)
        S = S * jnp.exp(gl_i)[..., None, None] + jnp.einsum('bhck,bhcv->bhkv', kdec_i, v_new)
        return S, o

    xs = tuple(jnp.moveaxis(t, 1, 0) for t in (u, w, qk, qg, kdec, glast))
    S_fin, o = lax.scan(step, S0.astype(f32), xs)
    o = jnp.moveaxis(o, 0, 1).swapaxes(2, 3).reshape(B, L, H, DV)
    return o, S_fin


def gdn_mixer(h, S0, conv0, w_in, conv_w, dt_bias, a_log, onorm_g, w_out):
    B, L, _ = h.shape
    proj = h @ w_in
    qkv = proj[..., :GDN_CONV_DIM]
    z = proj[..., GDN_CONV_DIM:GDN_CONV_DIM + GDN_V_DIM]
    b = proj[..., GDN_CONV_DIM + GDN_V_DIM:GDN_CONV_DIM + GDN_V_DIM + GDN_HEADS]
    a = proj[..., GDN_CONV_DIM + GDN_V_DIM + GDN_HEADS:]
    qkv, conv_new = causal_conv(qkv, conv0, conv_w)
    qkv = jax.nn.silu(qkv)
    q = l2norm(qkv[..., :GDN_QK_DIM].reshape(B, L, GDN_HEADS, GDN_DK))
    k = l2norm(qkv[..., GDN_QK_DIM:2 * GDN_QK_DIM].reshape(B, L, GDN_HEADS, GDN_DK))
    v = qkv[..., 2 * GDN_QK_DIM:].reshape(B, L, GDN_HEADS, GDN_DV)
    beta = jax.nn.sigmoid(b.astype(jnp.float32))
    g = -jnp.exp(a_log.astype(jnp.float32)) * jax.nn.softplus(a.astype(jnp.float32) + dt_bias.astype(jnp.float32))
    o, S_new = gated_delta_chunked(q, k, v, g, beta, S0)
    zf = z.reshape(B, L, GDN_HEADS, GDN_DV).astype(jnp.float32)
    o = rmsnorm(o, onorm_g) * jax.nn.silu(zf)
    y = o.reshape(B, L, GDN_V_DIM).astype(h.dtype) @ w_out
    return y, S_new.astype(S0.dtype), conv_new


def mla_shared_kv(x, cos, sin, kvin_g, w_dkv, kva_g):
    h = rmsnorm(x, kvin_g)
    kv = h @ w_dkv
    c_kv = rmsnorm(kv[..., :KV_LORA], kva_g)
    k_pe = apply_rope(kv[..., KV_LORA:], cos, sin)
    return c_kv, k_pe


def latent_attention(q_lat, q_pe, kc, kpe, q_pos, k_pos):
    B, L, H, _ = q_lat.shape
    qb = Q_BLOCK if L % Q_BLOCK == 0 else L
    nb = L // qb
    scale = QK_HEAD_DIM ** -0.5

    def blk(args):
        ql, qp, pos = args
        s = (jnp.einsum('bqhc,bkc->bhqk', ql, kc) + jnp.einsum('bqhr,bkr->bhqk', qp, kpe)).astype(jnp.float32) * scale
        s = jnp.where(k_pos[None, None, None, :] <= pos[None, None, :, None], s, -jnp.inf)
        p = jax.nn.softmax(s, axis=-1).astype(kc.dtype)
        return jnp.einsum('bhqk,bkc->bqhc', p, kc)

    xs = (q_lat.reshape(B, nb, qb, H, KV_LORA).swapaxes(0, 1),
          q_pe.reshape(B, nb, qb, H, ROPE_DIM).swapaxes(0, 1),
          q_pos.reshape(nb, qb))
    out = lax.map(blk, xs)
    return out.swapaxes(0, 1).reshape(B, L, H, KV_LORA)


def mla_mixer(h, cos, sin, kc, kpe, q_pos, k_pos, w_dq, qa_g, w_uq, w_uk, w_uv, w_o):
    B, L, _ = h.shape
    q = (rmsnorm(h @ w_dq, qa_g) @ w_uq).reshape(B, L, MLA_HEADS, QK_HEAD_DIM)
    q_nope, q_pe = q[..., :NOPE_DIM], q[..., NOPE_DIM:]
    q_pe = apply_rope(q_pe, cos, sin)
    q_lat = jnp.einsum('blhn,chn->blhc', q_nope, w_uk.reshape(KV_LORA, MLA_HEADS, NOPE_DIM))
    o_lat = latent_attention(q_lat, q_pe, kc, kpe, q_pos, k_pos)
    o = jnp.einsum('blhc,chv->blhv', o_lat, w_uv.reshape(KV_LORA, MLA_HEADS, V_HEAD_DIM))
    return o.reshape(B, L, MLA_HEADS * V_HEAD_DIM) @ w_o


def run_group(x, gdn_S0, conv0, past_c, past_pe, norm_pre, norm_post, ffn_in, ffn_out,
              gdn_w_in, gdn_conv_w, gdn_dt_bias, gdn_a_log, gdn_norm, gdn_w_out,
              mla_kvin_norm, mla_w_dkv, mla_kv_norm, mla_w_uk, mla_w_uv,
              mla_w_dq, mla_q_norm, mla_w_uq, mla_w_o):
    L = x.shape[1]
    past_len = past_c.shape[1]
    q_pos = past_len + jnp.arange(L, dtype=jnp.int32)
    k_pos = jnp.arange(past_len + L, dtype=jnp.int32)
    cos, sin = rope_tables(q_pos)
    states, convs = [], []
    c_new = pe_new = kc = kpe = None
    for layer in range(DEPTH):
        if layer == N_A_LAYERS:
            c_new, pe_new = mla_shared_kv(x, cos, sin, mla_kvin_norm, mla_w_dkv, mla_kv_norm)
            kc = jnp.concatenate([past_c.astype(c_new.dtype), c_new], axis=1)
            kpe = jnp.concatenate([past_pe.astype(pe_new.dtype), pe_new], axis=1)
        x = x + 0.5 * rmsnorm(swiglu(rmsnorm(x, norm_pre[layer, 0]), ffn_in[layer, 0], ffn_out[layer, 0]), norm_post[layer, 0])
        h = rmsnorm(x, norm_pre[layer, 1])
        if layer < N_A_LAYERS:
            y, S_new, cv_new = gdn_mixer(h, gdn_S0[layer], conv0[layer], gdn_w_in[layer], gdn_conv_w[layer],
                                         gdn_dt_bias[layer], gdn_a_log[layer], gdn_norm[layer], gdn_w_out[layer])
            states.append(S_new)
            convs.append(cv_new)
        else:
            j = layer - N_A_LAYERS
            y = mla_mixer(h, cos, sin, kc, kpe, q_pos, k_pos, mla_w_dq[j], mla_q_norm[j], mla_w_uq[j],
                          mla_w_uk, mla_w_uv, mla_w_o[j])
        x = x + rmsnorm(y, norm_post[layer, 1])
        x = x + 0.5 * rmsnorm(swiglu(rmsnorm(x, norm_pre[layer, 2]), ffn_in[layer, 1], ffn_out[layer, 1]), norm_post[layer, 2])
    return x, jnp.stack(states), jnp.stack(convs), c_new, pe_new


def setup_inputs(seed: int = 0) -> dict:
    key = jax.random.key(seed)
    ks = jax.random.split(key, 32)
    f32 = jnp.float32

    def nrm(k, shape, fan_in):
        return jax.random.normal(k, shape, f32) * (fan_in ** -0.5)

    n_pages = PAST_LEN // PAGE_SIZE
    n_used = DEC_BATCH * n_pages
    n_phys = n_used + n_used // 4
    perm = jax.random.permutation(ks[0], n_phys)
    page_table = perm[:n_used].reshape(DEC_BATCH, n_pages).astype(jnp.int32)

    return {
        "x_prompt": jax.random.normal(ks[1], (BATCH, SEQ, D_MODEL), f32),
        "x_sample": jax.random.normal(ks[2], (DEC_BATCH, DEC_SEQ, D_MODEL), f32),
        "state_gdn": nrm(ks[3], (N_A_LAYERS, DEC_BATCH, GDN_HEADS, GDN_DK, GDN_DV), GDN_DK),
        "state_conv": jax.random.normal(ks[4], (N_A_LAYERS, DEC_BATCH, CONV_W - 1, GDN_CONV_DIM), f32),
        "cache_ckv": jax.random.normal(ks[5], (n_phys, PAGE_SIZE, KV_LORA), f32),
        "cache_kpe": jax.random.normal(ks[6], (n_phys, PAGE_SIZE, ROPE_DIM), f32),
        "page_table": page_table,
        "norm_pre": 1.0 + 0.02 * jax.random.normal(ks[7], (DEPTH, 3, D_MODEL), f32),
        "norm_post": 1.0 + 0.02 * jax.random.normal(ks[8], (DEPTH, 3, D_MODEL), f32),
        "ffn_in": nrm(ks[9], (DEPTH, 2, D_MODEL, 2 * FFN_DIM), D_MODEL),
        "ffn_out": nrm(ks[10], (DEPTH, 2, FFN_DIM, D_MODEL), FFN_DIM),
        "gdn_w_in": nrm(ks[11], (N_A_LAYERS, D_MODEL, GDN_IN_DIM), D_MODEL),
        "gdn_conv_w": nrm(ks[12], (N_A_LAYERS, CONV_W, GDN_CONV_DIM), CONV_W),
        "gdn_dt_bias": 0.1 * jax.random.normal(ks[13], (N_A_LAYERS, GDN_HEADS), f32),
        "gdn_a_log": jnp.log(jax.random.uniform(ks[14], (N_A_LAYERS, GDN_HEADS), f32, 1.0, 16.0)),
        "gdn_norm": 1.0 + 0.02 * jax.random.normal(ks[15], (N_A_LAYERS, GDN_DV), f32),
        "gdn_w_out": nrm(ks[16], (N_A_LAYERS, GDN_V_DIM, D_MODEL), GDN_V_DIM),
        "mla_kvin_norm": 1.0 + 0.02 * jax.random.normal(ks[17], (D_MODEL,), f32),
        "mla_w_dkv": nrm(ks[18], (D_MODEL, KV_LORA + ROPE_DIM), D_MODEL),
        "mla_kv_norm": 1.0 + 0.02 * jax.random.normal(ks[19], (KV_LORA,), f32),
        "mla_w_uk": nrm(ks[20], (KV_LORA, MLA_HEADS * NOPE_DIM), KV_LORA),
        "mla_w_uv": nrm(ks[21], (KV_LORA, MLA_HEADS * V_HEAD_DIM), KV_LORA),
        "mla_w_dq": nrm(ks[22], (N_B_LAYERS, D_MODEL, Q_LORA), D_MODEL),
        "mla_q_norm": 1.0 + 0.02 * jax.random.normal(ks[23], (N_B_LAYERS, Q_LORA), f32),
        "mla_w_uq": nrm(ks[24], (N_B_LAYERS, Q_LORA, MLA_HEADS * QK_HEAD_DIM), Q_LORA),
        "mla_w_o": nrm(ks[25], (N_B_LAYERS, MLA_HEADS * V_HEAD_DIM, D_MODEL), MLA_HEADS * V_HEAD_DIM),
    }


def reference(x_prompt, x_sample, state_gdn, state_conv, cache_ckv, cache_kpe, page_table,
              norm_pre, norm_post, ffn_in, ffn_out,
              gdn_w_in, gdn_conv_w, gdn_dt_bias, gdn_a_log, gdn_norm, gdn_w_out,
              mla_kvin_norm, mla_w_dkv, mla_kv_norm, mla_w_uk, mla_w_uv,
              mla_w_dq, mla_q_norm, mla_w_uq, mla_w_o):
    B, S, _ = x_prompt.shape
    DB = x_sample.shape[0]
    past_len = page_table.shape[1] * PAGE_SIZE
    zero_S = jnp.zeros((N_A_LAYERS, B, GDN_HEADS, GDN_DK, GDN_DV), state_gdn.dtype)
    zero_conv = jnp.zeros((N_A_LAYERS, B, CONV_W - 1, GDN_CONV_DIM), state_conv.dtype)
    empty_c = jnp.zeros((B, 0, KV_LORA), cache_ckv.dtype)
    empty_pe = jnp.zeros((B, 0, ROPE_DIM), cache_kpe.dtype)
    past_c = cache_ckv[page_table].reshape(DB, past_len, KV_LORA)
    past_pe = cache_kpe[page_table].reshape(DB, past_len, ROPE_DIM)

    y_prompt, sg_p, cv_p, ckv_p, kpe_p = run_group(
        x_prompt, zero_S, zero_conv, empty_c, empty_pe, norm_pre, norm_post, ffn_in, ffn_out,
        gdn_w_in, gdn_conv_w, gdn_dt_bias, gdn_a_log, gdn_norm, gdn_w_out,
        mla_kvin_norm, mla_w_dkv, mla_kv_norm, mla_w_uk, mla_w_uv,
        mla_w_dq, mla_q_norm, mla_w_uq, mla_w_o)
    y_sample, sg_s, cv_s, ckv_s, kpe_s = run_group(
        x_sample, state_gdn, state_conv, past_c, past_pe, norm_pre, norm_post, ffn_in, ffn_out,
        gdn_w_in, gdn_conv_w, gdn_dt_bias, gdn_a_log, gdn_norm, gdn_w_out,
        mla_kvin_norm, mla_w_dkv, mla_kv_norm, mla_w_uk, mla_w_uv,
        mla_w_dq, mla_q_norm, mla_w_uq, mla_w_o)
    return (y_prompt, y_sample, sg_p, cv_p, ckv_p, kpe_p, sg_s, cv_s, ckv_s, kpe_s)
```

```python
import functools
import math

import jax
import jax.numpy as jnp
from jax import lax
from jax.experimental import pallas as pl
from jax.experimental.pallas import tpu as pltpu

EPS = 1e-6
ROPE_THETA = 10000.0
GDN_HEADS = 8
GDN_HEAD_DIM = 128
CONV_W = 4
MLA_HEADS = 8
KV_LORA = 256
NOPE_DIM = 128
ROPE_DIM = 64
V_HEAD_DIM = 128
PAGE_SIZE = 128

LANES = 128
SUBLANES = 8
VMEM_LIMIT_BYTES = 56 * 1024 * 1024
GDN_CHUNK_ROWS = 128
NEG_BIG = -1e30

bf16 = jnp.bfloat16
f32 = jnp.float32


def _cparams(*sem):
    return pltpu.CompilerParams(dimension_semantics=sem, vmem_limit_bytes=VMEM_LIMIT_BYTES)


def _rms(x, g):
    return x * lax.rsqrt(jnp.mean(x * x, axis=-1, keepdims=True) + EPS) * g


def _sigmoid(x):
    return 1.0 / (1.0 + jnp.exp(-x))


def _mm(a, b):
    return jnp.dot(a.astype(bf16), b.astype(bf16), preferred_element_type=f32)


def _mm_nt(a, b):
    return lax.dot_general(a.astype(bf16), b.astype(bf16), (((1,), (1,)), ((), ())),
                           preferred_element_type=f32)


def _mm_tn(a, b):
    return lax.dot_general(a.astype(bf16), b.astype(bf16), (((0,), (0,)), ((), ())),
                           preferred_element_type=f32)


def _row_tile(m, cap):
    t = min(m, cap)
    while m % t:
        t //= 2
    return t


def _ffn_kernel(x_ref, gpre_ref, wg_ref, wu_ref, wo_ref, gpost_ref, o_ref, h_scr, acc_scr):
    j = pl.program_id(1)

    @pl.when(j == 0)
    def _():
        h_scr[...] = _rms(x_ref[...], gpre_ref[...]).astype(bf16)
        acc_scr[...] = jnp.zeros_like(acc_scr)

    h = h_scr[...]
    g = jnp.dot(h, wg_ref[...], preferred_element_type=f32)
    u = jnp.dot(h, wu_ref[...], preferred_element_type=f32)
    a = (g * _sigmoid(g) * u).astype(bf16)
    acc_scr[...] += jnp.dot(a, wo_ref[...], preferred_element_type=f32)

    @pl.when(j == pl.num_programs(1) - 1)
    def _():
        o_ref[...] = x_ref[...] + 0.5 * _rms(acc_scr[...], gpost_ref[...])


def _ffn_chunk(f):
    for c in (256, 128):
        if f % c == 0:
            return c
    raise ValueError(f"FFN width {f} is not a multiple of {LANES}")


def ffn_sublayer(x, g_pre, w_in, w_out, g_post, layer, k):
    m, d = x.shape
    f = w_out.shape[2]
    tm = _row_tile(m, 1024)
    fc = _ffn_chunk(f)
    nf = f // fc
    return pl.pallas_call(
        _ffn_kernel,
        grid=(m // tm, nf),
        in_specs=[
            pl.BlockSpec((tm, d), lambda i, j: (i, 0)),
            pl.BlockSpec((1, d), lambda i, j: (0, 0)),
            pl.BlockSpec((None, None, d, fc), lambda i, j: (layer, k, 0, j)),
            pl.BlockSpec((None, None, d, fc), lambda i, j: (layer, k, 0, j + nf)),
            pl.BlockSpec((None, None, fc, d), lambda i, j: (layer, k, j, 0)),
            pl.BlockSpec((1, d), lambda i, j: (0, 0)),
        ],
        out_specs=pl.BlockSpec((tm, d), lambda i, j: (i, 0)),
        out_shape=jax.ShapeDtypeStruct((m, d), f32),
        scratch_shapes=[pltpu.VMEM((tm, d), bf16), pltpu.VMEM((tm, d), f32)],
        compiler_params=_cparams("parallel", "arbitrary"),
        name="ffn_sublayer",
    )(x, g_pre.reshape(1, d), w_in, w_in, w_out, g_post.reshape(1, d))


def _norm_proj_kernel(x_ref, g_ref, w_ref, o_ref, h_scr):
    @pl.when(pl.program_id(1) == 0)
    def _():
        h_scr[...] = _rms(x_ref[...], g_ref[...]).astype(bf16)

    o_ref[...] = jnp.dot(h_scr[...], w_ref[...], preferred_element_type=f32).astype(o_ref.dtype)


def norm_proj(x, g, w):
    m, d = x.shape
    n = w.shape[1]
    tm = _row_tile(m, 512)
    tn = n // 2 if (n // 2) % LANES == 0 and n > 2304 else n
    return pl.pallas_call(
        _norm_proj_kernel,
        grid=(m // tm, n // tn),
        in_specs=[
            pl.BlockSpec((tm, d), lambda i, j: (i, 0)),
            pl.BlockSpec((1, d), lambda i, j: (0, 0)),
            pl.BlockSpec((d, tn), lambda i, j: (0, j)),
        ],
        out_specs=pl.BlockSpec((tm, tn), lambda i, j: (i, j)),
        out_shape=jax.ShapeDtypeStruct((m, n), f32),
        scratch_shapes=[pltpu.VMEM((tm, d), bf16)],
        compiler_params=_cparams("parallel", "arbitrary"),
        name="gdn_in_proj",
    )(x, g.reshape(1, d), w)


def _out_proj_kernel(a_ref, w_ref, x_ref, g_ref, o_ref):
    y = jnp.dot(a_ref[...].astype(bf16), w_ref[...], preferred_element_type=f32)
    o_ref[...] = x_ref[...] + _rms(y, g_ref[...])


def out_proj_residual(a, w, x, g_post):
    m, d = x.shape
    k = a.shape[1]
    tm = _row_tile(m, 512)
    return pl.pallas_call(
        _out_proj_kernel,
        grid=(m // tm,),
        in_specs=[
            pl.BlockSpec((tm, k), lambda i: (i, 0)),
            pl.BlockSpec((k, d), lambda i: (0, 0)),
            pl.BlockSpec((tm, d), lambda i: (i, 0)),
            pl.BlockSpec((1, d), lambda i: (0, 0)),
        ],
        out_specs=pl.BlockSpec((tm, d), lambda i: (i, 0)),
        out_shape=jax.ShapeDtypeStruct((m, d), f32),
        compiler_params=_cparams("parallel"),
        name="out_proj_residual",
    )(a, w, x, g_post.reshape(1, d))


def _gdn_kernel(proj_ref, convw_ref, conv0_ref, s0_ref, dtb_ref, alog_ref, og_ref,
                o_ref, s_out_ref, s_scr, xp_scr, *, heads, rows):
    n = pl.program_id(1)
    hd = GDN_HEAD_DIM
    c = GDN_CHUNK_ROWS
    qk_dim = heads * hd
    conv_dim = 3 * qk_dim
    z_off = conv_dim
    b_off = z_off + qk_dim
    a_off = b_off + LANES
    pad = SUBLANES

    @pl.when(n == 0)
    def _():
        s_scr[...] = s0_ref[0]
        xp_scr[0:pad, :] = conv0_ref[0]
        if rows < c:
            xp_scr[pad + rows:pad + c, :] = jnp.zeros((c - rows, conv_dim), f32)

    if rows == c:
        @pl.when(n > 0)
        def _():
            xp_scr[0:pad, :] = xp_scr[c:c + pad, :]

    xp_scr[pad:pad + rows, :] = proj_ref[0, :, 0:conv_dim]

    def gate_block(off):
        blk = proj_ref[0, :, off:off + LANES]
        if rows < c:
            blk = jnp.concatenate([blk, jnp.zeros((c - rows, LANES), f32)], axis=0)
        return blk

    row_i = lax.broadcasted_iota(jnp.int32, (c, c), 0)
    col_i = lax.broadcasted_iota(jnp.int32, (c, c), 1)
    tril = row_i >= col_i
    strict = row_i > col_i
    real = lax.broadcasted_iota(jnp.int32, (c, LANES), 0) < rows

    beta = jnp.where(real, _sigmoid(gate_block(b_off)), 0.0)
    a_in = gate_block(a_off) + dtb_ref[...]
    softplus = jnp.maximum(a_in, 0.0) + jnp.log(1.0 + jnp.exp(-jnp.abs(a_in)))
    g = jnp.where(real, -jnp.exp(alog_ref[...]) * softplus, 0.0)
    gcum = jnp.dot(tril.astype(f32), g, preferred_element_type=f32, precision=lax.Precision.HIGHEST)
    gcum_t = gcum.T
    glast = gcum[c - 1:c, :]
    e_g = jnp.exp(gcum)
    e_kd = jnp.exp(glast - gcum)
    e_last = jnp.exp(glast)

    def conv_silu(col0):
        acc = xp_scr[pad - 3:pad - 3 + c, col0:col0 + hd] * convw_ref[0:1, col0:col0 + hd]
        for j in range(1, CONV_W):
            acc = acc + xp_scr[pad - 3 + j:pad - 3 + j + c, col0:col0 + hd] * convw_ref[j:j + 1, col0:col0 + hd]
        return acc * _sigmoid(acc)

    def l2norm(t):
        return t * lax.rsqrt(jnp.sum(t * t, axis=-1, keepdims=True) + EPS)

    def col(t, h):
        return jnp.broadcast_to(t[:, h:h + 1], (t.shape[0], hd))

    n_iter = max(int(math.ceil(math.log2(rows))) - 1, 0)

    for h in range(heads):
        qh = l2norm(conv_silu(h * hd)) * (hd ** -0.5)
        kh = l2norm(conv_silu(qk_dim + h * hd))
        vh = conv_silu(2 * qk_dim + h * hd)
        beta_c = col(beta, h)
        diff = col(gcum, h) - jnp.broadcast_to(gcum_t[h:h + 1, :], (c, c))
        decay = jnp.where(tril, jnp.exp(jnp.where(tril, diff, 0.0)), 0.0)
        kb = kh * beta_c
        lmat = jnp.where(strict, _mm_nt(kb, kh) * decay, 0.0)
        pw = -lmat
        nm = pw
        for _ in range(n_iter):
            pw = _mm(pw, pw)
            nm = nm + pw + _mm(nm, pw)
        rhs = jnp.concatenate([vh * beta_c, kb * col(e_g, h)], axis=1)
        sol = rhs + _mm(nm, rhs)
        u = sol[:, :hd]
        w = sol[:, hd:]
        qk = _mm_nt(qh, kh) * decay
        qg = qh * col(e_g, h)
        kdec = kh * col(e_kd, h)
        s_prev = s_scr[h]
        v_new = u - _mm(w, s_prev)
        o = _mm(qg, s_prev) + _mm(qk, v_new)
        s_scr[h] = s_prev * e_last[:, h:h + 1] + _mm_tn(kdec, v_new)
        zh = proj_ref[0, :, z_off + h * hd:z_off + (h + 1) * hd]
        o_ref[0, :, h * hd:(h + 1) * hd] = (_rms(o[:rows], og_ref[...]) * (zh * _sigmoid(zh))).astype(o_ref.dtype)

    @pl.when(n == pl.num_programs(1) - 1)
    def _():
        s_out_ref[0] = s_scr[...]


def gdn_core(proj, conv_w, conv0, s0, dt_bias, a_log, onorm_g, out_dtype):
    b, l, p = proj.shape
    heads = s0.shape[1]
    hd = GDN_HEAD_DIM
    conv_dim = 3 * heads * hd
    rows = min(l, GDN_CHUNK_ROWS)
    assert l % rows == 0 and (rows == GDN_CHUNK_ROWS or l == rows)
    conv0p = jnp.concatenate([jnp.zeros((b, SUBLANES - (CONV_W - 1), conv_dim), f32), conv0.astype(f32)], axis=1)
    pad_l = lambda t: jnp.pad(t.astype(f32), (0, LANES - heads)).reshape(1, LANES)
    kern = functools.partial(_gdn_kernel, heads=heads, rows=rows)
    return pl.pallas_call(
        kern,
        grid=(b, l // rows),
        in_specs=[
            pl.BlockSpec((1, rows, p), lambda i, n: (i, n, 0)),
            pl.BlockSpec((CONV_W, conv_dim), lambda i, n: (0, 0)),
            pl.BlockSpec((1, SUBLANES, conv_dim), lambda i, n: (i, 0, 0)),
            pl.BlockSpec((1, heads, hd, hd), lambda i, n: (i, 0, 0, 0)),
            pl.BlockSpec((1, LANES), lambda i, n: (0, 0)),
            pl.BlockSpec((1, LANES), lambda i, n: (0, 0)),
            pl.BlockSpec((1, hd), lambda i, n: (0, 0)),
        ],
        out_specs=[
            pl.BlockSpec((1, rows, heads * hd), lambda i, n: (i, n, 0)),
            pl.BlockSpec((1, heads, hd, hd), lambda i, n: (i, 0, 0, 0)),
        ],
        out_shape=[
            jax.ShapeDtypeStruct((b, l, heads * hd), out_dtype),
            jax.ShapeDtypeStruct(s0.shape, s0.dtype),
        ],
        scratch_shapes=[
            pltpu.VMEM((heads, hd, hd), f32),
            pltpu.VMEM((GDN_CHUNK_ROWS + SUBLANES, conv_dim), f32),
        ],
        compiler_params=_cparams("parallel", "arbitrary"),
        name="gdn_core",
    )(proj, conv_w, conv0p, s0, pad_l(dt_bias), pad_l(a_log), onorm_g.reshape(1, hd))


def _mla_kv_kernel(x_ref, gin_ref, wc_ref, wpa_ref, wpb_ref, gkv_ref, cos_ref, sin_ref,
                   c_ref, pe_ref, cb_ref, peb_ref):
    h = _rms(x_ref[...], gin_ref[...]).astype(bf16)
    ckv = _rms(jnp.dot(h, wc_ref[...], preferred_element_type=f32), gkv_ref[...])
    pa = jnp.dot(h, wpa_ref[...], preferred_element_type=f32)
    pb = jnp.dot(h, wpb_ref[...], preferred_element_type=f32)
    kpe = pa * cos_ref[...] + pb * sin_ref[...]
    c_ref[...] = ckv
    pe_ref[...] = kpe
    cb_ref[...] = ckv.astype(bf16)
    peb_ref[...] = kpe.astype(bf16)


def _swap_halves(w):
    half = w.shape[-1] // 2
    return jnp.concatenate([w[..., half:], w[..., :half]], axis=-1)


def mla_shared_kv(x, g_in, w_dkv, g_kv, cos2, sin2):
    m, d = x.shape
    tm = _row_tile(cos2.shape[0], 512)
    nt = cos2.shape[0] // tm
    w_c = w_dkv[:, :KV_LORA].astype(bf16)
    w_pa = w_dkv[:, KV_LORA:].astype(bf16)
    w_pb = _swap_halves(w_pa)
    full = lambda r, c: pl.BlockSpec((r, c), lambda i: (0, 0))
    rows = lambda c: pl.BlockSpec((tm, c), lambda i: (i, 0))
    tab = pl.BlockSpec((tm, ROPE_DIM), lambda i: (i % nt, 0))
    return pl.pallas_call(
        _mla_kv_kernel,
        grid=(m // tm,),
        in_specs=[rows(d), full(1, d), full(d, KV_LORA), full(d, ROPE_DIM), full(d, ROPE_DIM),
                  full(1, KV_LORA), tab, tab],
        out_specs=[rows(KV_LORA), rows(ROPE_DIM), rows(KV_LORA), rows(ROPE_DIM)],
        out_shape=[
            jax.ShapeDtypeStruct((m, KV_LORA), f32),
            jax.ShapeDtypeStruct((m, ROPE_DIM), f32),
            jax.ShapeDtypeStruct((m, KV_LORA), bf16),
            jax.ShapeDtypeStruct((m, ROPE_DIM), bf16),
        ],
        compiler_params=_cparams("parallel"),
        name="mla_shared_kv",
    )(x, g_in.reshape(1, d), w_c, w_pa, w_pb, g_kv.reshape(1, KV_LORA), cos2, sin2)


def _mla_q_kernel(x_ref, gpre_ref, wdq_ref, gq_ref, wn_ref, wpa_ref, wpb_ref, wuk_ref, cos_ref, sin_ref,
                  qlat_ref, qpe_ref, *, heads, scale):
    h = _rms(x_ref[...], gpre_ref[...]).astype(bf16)
    qa = _rms(jnp.dot(h, wdq_ref[...], preferred_element_type=f32), gq_ref[...]).astype(bf16)
    nope = jnp.dot(qa, wn_ref[...], preferred_element_type=f32).astype(bf16)
    pa = jnp.dot(qa, wpa_ref[...], preferred_element_type=f32)
    pb = jnp.dot(qa, wpb_ref[...], preferred_element_type=f32)
    qpe = (pa * cos_ref[...] + pb * sin_ref[...]) * scale
    for hh in range(heads):
        qlat = jnp.dot(nope[:, hh * NOPE_DIM:(hh + 1) * NOPE_DIM], wuk_ref[hh], preferred_element_type=f32)
        qlat_ref[hh] = (qlat * scale).astype(qlat_ref.dtype)
        qpe_ref[hh] = qpe[:, hh * ROPE_DIM:(hh + 1) * ROPE_DIM].astype(qpe_ref.dtype)


def mla_query(x, g_pre, w_dq, g_q, w_uq, w_uk, cos2h, sin2h, out_dtype):
    m, d = x.shape
    heads = MLA_HEADS
    q_lora = w_dq.shape[1]
    qk_head = NOPE_DIM + ROPE_DIM
    tm = _row_tile(cos2h.shape[0], 512)
    nt = cos2h.shape[0] // tm
    w3 = w_uq.reshape(q_lora, heads, qk_head)
    w_n = w3[:, :, :NOPE_DIM].reshape(q_lora, heads * NOPE_DIM).astype(bf16)
    w_pa3 = w3[:, :, NOPE_DIM:]
    w_pa = w_pa3.reshape(q_lora, heads * ROPE_DIM).astype(bf16)
    w_pb = _swap_halves(w_pa3).reshape(q_lora, heads * ROPE_DIM).astype(bf16)
    w_ukt = jnp.transpose(w_uk.reshape(KV_LORA, heads, NOPE_DIM), (1, 2, 0)).astype(bf16)
    scale = qk_head ** -0.5
    full = lambda *s: pl.BlockSpec(s, lambda i: (0,) * len(s))
    tab = pl.BlockSpec((tm, heads * ROPE_DIM), lambda i: (i % nt, 0))
    kern = functools.partial(_mla_q_kernel, heads=heads, scale=scale)
    return pl.pallas_call(
        kern,
        grid=(m // tm,),
        in_specs=[pl.BlockSpec((tm, d), lambda i: (i, 0)), full(1, d), full(d, q_lora), full(1, q_lora),
                  full(q_lora, heads * NOPE_DIM), full(q_lora, heads * ROPE_DIM), full(q_lora, heads * ROPE_DIM),
                  full(heads, NOPE_DIM, KV_LORA), tab, tab],
        out_specs=[pl.BlockSpec((heads, tm, KV_LORA), lambda i: (0, i, 0)),
                   pl.BlockSpec((heads, tm, ROPE_DIM), lambda i: (0, i, 0))],
        out_shape=[jax.ShapeDtypeStruct((heads, m, KV_LORA), out_dtype),
                   jax.ShapeDtypeStruct((heads, m, ROPE_DIM), out_dtype)],
        compiler_params=_cparams("parallel"),
        name="mla_query",
    )(x, g_pre.reshape(1, d), w_dq.astype(bf16), g_q.reshape(1, q_lora), w_n, w_pa, w_pb, w_ukt, cos2h, sin2h)


def _softmax_update(s, pv_fn, m_scr, l_scr, acc_scr):
    m_prev = m_scr[...]
    m_new = jnp.maximum(m_prev, jnp.max(s, axis=-1, keepdims=True))
    alpha = jnp.exp(m_prev - m_new)
    p = jnp.exp(s - m_new[:, 0:1])
    l_scr[...] = alpha * l_scr[...] + jnp.sum(p, axis=-1, keepdims=True)
    acc_scr[...] = acc_scr[...] * alpha[:, 0:1] + pv_fn(p.astype(bf16))
    m_scr[...] = m_new


def _softmax_init(m_scr, l_scr, acc_scr):
    m_scr[...] = jnp.full_like(m_scr, NEG_BIG)
    l_scr[...] = jnp.zeros_like(l_scr)
    acc_scr[...] = jnp.zeros_like(acc_scr)


def _prompt_attn_kernel(qlat_ref, qpe_ref, kc_ref, kpe_ref, wuv_ref, o_ref, m_scr, l_scr, acc_scr,
                        *, heads, qb, tk):
    i = pl.program_id(1)
    j = pl.program_id(2)
    r = heads * qb

    @pl.when(j == 0)
    def _():
        _softmax_init(m_scr, l_scr, acc_scr)

    @pl.when(j * tk <= i * qb + qb - 1)
    def _():
        q = qlat_ref[...].reshape(r, KV_LORA)
        qp = qpe_ref[...].reshape(r, ROPE_DIM)
        kc = kc_ref[...]
        s = _mm_nt(q, kc) + _mm_nt(qp, kpe_ref[...])
        q_pos = i * qb + lax.rem(lax.broadcasted_iota(jnp.int32, (r, tk), 0), jnp.int32(qb))
        k_pos = j * tk + lax.broadcasted_iota(jnp.int32, (r, tk), 1)
        s = jnp.where(k_pos <= q_pos, s, NEG_BIG)
        _softmax_update(s, lambda p: jnp.dot(p, kc, preferred_element_type=f32), m_scr, l_scr, acc_scr)

    @pl.when(j == pl.num_programs(2) - 1)
    def _():
        o_lat = (acc_scr[...] / l_scr[:, 0:1]).astype(bf16)
        for hh in range(heads):
            o_ref[:, hh * V_HEAD_DIM:(hh + 1) * V_HEAD_DIM] = jnp.dot(
                o_lat[hh * qb:(hh + 1) * qb], wuv_ref[hh], preferred_element_type=f32).astype(o_ref.dtype)


def prompt_attention(q_lat, q_pe, kc, kpe, w_uv3, batch, seq):
    heads = q_lat.shape[0]
    qb = _row_tile(seq, 256)
    tk = qb
    nq = seq // qb
    nk = seq // tk
    r = heads * qb

    def kv_map(b, i, j):
        return (b * nk + jnp.minimum(j, (i * qb + qb - 1) // tk), 0)

    kern = functools.partial(_prompt_attn_kernel, heads=heads, qb=qb, tk=tk)
    return pl.pallas_call(
        kern,
        grid=(batch, nq, nk),
        in_specs=[
            pl.BlockSpec((heads, qb, KV_LORA), lambda b, i, j: (0, b * nq + i, 0)),
            pl.BlockSpec((heads, qb, ROPE_DIM), lambda b, i, j: (0, b * nq + i, 0)),
            pl.BlockSpec((tk, KV_LORA), kv_map),
            pl.BlockSpec((tk, ROPE_DIM), kv_map),
            pl.BlockSpec((heads, KV_LORA, V_HEAD_DIM), lambda b, i, j: (0, 0, 0)),
        ],
        out_specs=pl.BlockSpec((qb, heads * V_HEAD_DIM), lambda b, i, j: (b * nq + i, 0)),
        out_shape=jax.ShapeDtypeStruct((batch * seq, heads * V_HEAD_DIM), bf16),
        scratch_shapes=[pltpu.VMEM((r, LANES), f32), pltpu.VMEM((r, LANES), f32), pltpu.VMEM((r, KV_LORA), f32)],
        compiler_params=_cparams("parallel", "parallel", "arbitrary"),
        name="prompt_attention",
    )(q_lat, q_pe, kc, kpe, w_uv3)


def _decode_attn_kernel(pt_ref, qlat_ref, qpe_ref, cnew_ref, penew_ref, wuv_ref, *rest, heads, seq, pages):
    ck_refs = rest[:pages]
    pe_refs = rest[pages:2 * pages]
    o_ref, m_scr, l_scr, acc_scr = rest[2 * pages:]
    j = pl.program_id(1)
    last = pl.num_programs(1) - 1
    r = heads * seq
    q = qlat_ref[...].reshape(r, KV_LORA).astype(bf16)
    qp = qpe_ref[...].reshape(r, ROPE_DIM).astype(bf16)

    @pl.when(j == 0)
    def _():
        _softmax_init(m_scr, l_scr, acc_scr)

    @pl.when(j < last)
    def _():
        kcs = [ck_refs[p][0].astype(bf16) for p in range(pages)]
        s = jnp.concatenate(
            [_mm_nt(q, kcs[p]) + _mm_nt(qp, pe_refs[p][0]) for p in range(pages)], axis=1)

        def pv(pm):
            acc = jnp.dot(pm[:, 0:PAGE_SIZE], kcs[0], preferred_element_type=f32)
            for p in range(1, pages):
                acc = acc + jnp.dot(pm[:, p * PAGE_SIZE:(p + 1) * PAGE_SIZE], kcs[p], preferred_element_type=f32)
            return acc

        _softmax_update(s, pv, m_scr, l_scr, acc_scr)

    @pl.when(j == last)
    def _():
        kc = jnp.concatenate([cnew_ref[0], jnp.zeros((PAGE_SIZE - seq, KV_LORA), f32)], axis=0).astype(bf16)
        kp = jnp.concatenate([penew_ref[0], jnp.zeros((PAGE_SIZE - seq, ROPE_DIM), f32)], axis=0)
        s = _mm_nt(q, kc) + _mm_nt(qp, kp)
        q_t = lax.rem(lax.broadcasted_iota(jnp.int32, (r, PAGE_SIZE), 0), jnp.int32(seq))
        k_t = lax.broadcasted_iota(jnp.int32, (r, PAGE_SIZE), 1)
        s = jnp.where(k_t <= q_t, s, NEG_BIG)
        _softmax_update(s, lambda pm: jnp.dot(pm, kc, preferred_element_type=f32), m_scr, l_scr, acc_scr)
        o_lat = (acc_scr[...] / l_scr[:, 0:1]).astype(bf16)
        for hh in range(heads):
            full = jnp.dot(o_lat, wuv_ref[hh], preferred_element_type=f32)
            o_ref[:, hh * V_HEAD_DIM:(hh + 1) * V_HEAD_DIM] = full[hh * seq:(hh + 1) * seq]


def decode_attention(q_lat, q_pe, c_new, pe_new, cache_ckv, cache_kpe, page_table, w_uv3):
    heads = q_lat.shape[0]
    db, seq, _ = c_new.shape
    n_pages = page_table.shape[1]
    pages = 8 if n_pages % 8 == 0 and n_pages >= 16 else (4 if n_pages % 4 == 0 else 1)
    ns = n_pages // pages
    r = heads * seq

    def page_spec(p, width):
        return pl.BlockSpec((1, PAGE_SIZE, width),
                            lambda b, j, pt: (pt[b, jnp.minimum(j, ns - 1) * pages + p], 0, 0))

    kern = functools.partial(_decode_attn_kernel, heads=heads, seq=seq, pages=pages)
    grid_spec = pltpu.PrefetchScalarGridSpec(
        num_scalar_prefetch=1,
        grid=(db, ns + 1),
        in_specs=[
            pl.BlockSpec((heads, seq, KV_LORA), lambda b, j, pt: (0, b, 0)),
            pl.BlockSpec((heads, seq, ROPE_DIM), lambda b, j, pt: (0, b, 0)),
            pl.BlockSpec((1, seq, KV_LORA), lambda b, j, pt: (b, 0, 0)),
            pl.BlockSpec((1, seq, ROPE_DIM), lambda b, j, pt: (b, 0, 0)),
            pl.BlockSpec((heads, KV_LORA, V_HEAD_DIM), lambda b, j, pt: (0, 0, 0)),
        ] + [page_spec(p, KV_LORA) for p in range(pages)] + [page_spec(p, ROPE_DIM) for p in range(pages)],
        out_specs=pl.BlockSpec((seq, heads * V_HEAD_DIM), lambda b, j, pt: (b, 0)),
        scratch_shapes=[pltpu.VMEM((r, LANES), f32), pltpu.VMEM((r, LANES), f32), pltpu.VMEM((r, KV_LORA), f32)],
    )
    return pl.pallas_call(
        kern,
        grid_spec=grid_spec,
        out_shape=jax.ShapeDtypeStruct((db * seq, heads * V_HEAD_DIM), f32),
        compiler_params=_cparams("parallel", "arbitrary"),
        name="decode_attention",
    )(page_table, q_lat, q_pe, c_new, pe_new, w_uv3, *([cache_ckv] * pages), *([cache_kpe] * pages))


def _rope_tables(first_pos, seq):
    half = ROPE_DIM // 2
    inv_freq = ROPE_THETA ** (-jnp.arange(half, dtype=f32) / half)
    pos = first_pos + jnp.arange(seq, dtype=jnp.int32)
    ang = pos.astype(f32)[:, None] * inv_freq[None, :]
    cos, sin = jnp.cos(ang), jnp.sin(ang)
    return jnp.concatenate([cos, cos], axis=1), jnp.concatenate([-sin, sin], axis=1)


def _run_group(x3, gdn_s0, conv0, past, wts):
    batch, seq, d = x3.shape
    n_a = wts["gdn_w_in"].shape[0]
    depth = wts["ffn_in_b"].shape[0]
    x = x3.reshape(batch * seq, d)
    past_len = 0 if past is None else past[2].shape[1] * PAGE_SIZE
    cos2, sin2 = _rope_tables(past_len, seq)
    if seq < LANES:
        cos2, sin2 = jnp.tile(cos2, (batch, 1)), jnp.tile(sin2, (batch, 1))
    cos2h, sin2h = jnp.tile(cos2, (1, MLA_HEADS)), jnp.tile(sin2, (1, MLA_HEADS))
    act_dtype = bf16 if past is None else f32
    states, convs = [], []
    c_new = pe_new = c_bf = pe_bf = None
    for layer in range(depth):
        if layer == n_a:
            c_new, pe_new, c_bf, pe_bf = mla_shared_kv(
                x, wts["mla_kvin_norm"], wts["mla_w_dkv"], wts["mla_kv_norm"], cos2, sin2)
        x = ffn_sublayer(x, wts["norm_pre"][layer, 0], wts["ffn_in_b"], wts["ffn_out_b"],
                         wts["norm_post"][layer, 0], layer, 0)
        if layer < n_a:
            proj = norm_proj(x, wts["norm_pre"][layer, 1], wts["gdn_w_in_b"][layer])
            proj3 = proj.reshape(batch, seq, -1)
            conv_dim = conv0.shape[-1]
            o, s_new = gdn_core(proj3, wts["gdn_conv_w"][layer], conv0[layer], gdn_s0[layer],
                                wts["gdn_dt_bias"][layer], wts["gdn_a_log"][layer], wts["gdn_norm"][layer], act_dtype)
            states.append(s_new)
            if seq >= CONV_W - 1:
                convs.append(proj3[:, seq - (CONV_W - 1):, :conv_dim])
            else:
                prev = jnp.concatenate([conv0[layer].astype(f32), proj3[:, :, :conv_dim]], axis=1)
                convs.append(prev[:, -(CONV_W - 1):])
            x = out_proj_residual(o.reshape(batch * seq, -1), wts["gdn_w_out_b"][layer], x, wts["norm_post"][layer, 1])
        else:
            jb = layer - n_a
            q_lat, q_pe = mla_query(x, wts["norm_pre"][layer, 1], wts["mla_w_dq"][jb], wts["mla_q_norm"][jb],
                                    wts["mla_w_uq"][jb], wts["mla_w_uk"], cos2h, sin2h, act_dtype)
            if past is None:
                o = prompt_attention(q_lat, q_pe, c_bf, pe_bf, wts["w_uv3"], batch, seq)
            else:
                o = decode_attention(q_lat, q_pe, c_new.reshape(batch, seq, -1), pe_new.reshape(batch, seq, -1),
                                     past[0], past[1], past[2], wts["w_uv3"])
            x = out_proj_residual(o, wts["mla_w_o_b"][jb], x, wts["norm_post"][layer, 1])
        x = ffn_sublayer(x, wts["norm_pre"][layer, 2], wts["ffn_in_b"], wts["ffn_out_b"],
                         wts["norm_post"][layer, 2], layer, 1)
    return (x.reshape(batch, seq, d), jnp.stack(states), jnp.stack(convs),
            c_new.reshape(batch, seq, -1), pe_new.reshape(batch, seq, -1))


def _pack_gdn_w_in(w_in, heads):
    main = 4 * heads * GDN_HEAD_DIM
    padw = ((0, 0), (0, 0), (0, LANES - heads))
    b = jnp.pad(w_in[..., main:main + heads], padw)
    a = jnp.pad(w_in[..., main + heads:main + 2 * heads], padw)
    return jnp.concatenate([w_in[..., :main], b, a], axis=-1).astype(bf16)


def kernel(x_prompt, x_sample, state_gdn, state_conv, cache_ckv, cache_kpe, page_table, norm_pre, norm_post, ffn_in, ffn_out, gdn_w_in, gdn_conv_w, gdn_dt_bias, gdn_a_log, gdn_norm, gdn_w_out, mla_kvin_norm, mla_w_dkv, mla_kv_norm, mla_w_uk, mla_w_uv, mla_w_dq, mla_q_norm, mla_w_uq, mla_w_o):
    n_a = gdn_w_in.shape[0]
    b = x_prompt.shape[0]
    heads = state_gdn.shape[2]
    wts = dict(
        norm_pre=norm_pre, norm_post=norm_post,
        ffn_in_b=ffn_in.astype(bf16), ffn_out_b=ffn_out.astype(bf16),
        gdn_w_in=gdn_w_in, gdn_w_in_b=_pack_gdn_w_in(gdn_w_in, heads), gdn_conv_w=gdn_conv_w,
        gdn_dt_bias=gdn_dt_bias, gdn_a_log=gdn_a_log, gdn_norm=gdn_norm, gdn_w_out_b=gdn_w_out.astype(bf16),
        mla_kvin_norm=mla_kvin_norm, mla_w_dkv=mla_w_dkv, mla_kv_norm=mla_kv_norm, mla_w_uk=mla_w_uk,
        w_uv3=jnp.transpose(mla_w_uv.reshape(KV_LORA, MLA_HEADS, V_HEAD_DIM), (1, 0, 2)).astype(bf16),
        mla_w_dq=mla_w_dq, mla_q_norm=mla_q_norm, mla_w_uq=mla_w_uq, mla_w_o_b=mla_w_o.astype(bf16),
    )
    zero_s = jnp.zeros((n_a, b) + state_gdn.shape[2:], state_gdn.dtype)
    zero_conv = jnp.zeros((n_a, b) + state_conv.shape[2:], state_conv.dtype)
    y_p, sg_p, cv_p, ckv_p, kpe_p = _run_group(x_prompt, zero_s, zero_conv, None, wts)
    y_s, sg_s, cv_s, ckv_s, kpe_s = _run_group(x_sample, state_gdn, state_conv,
                                               (cache_ckv, cache_kpe, page_table), wts)
    return (y_p, y_s, sg_p, cv_p, ckv_p, kpe_p, sg_s, cv_s, ckv_s, kpe_s)
```

```python
import functools
import math

import jax
import jax.numpy as jnp
from jax import lax
from jax.experimental import pallas as pl
from jax.experimental.pallas import tpu as pltpu

EPS = 1e-6
ROPE_THETA = 10000.0
GDN_HEADS = 8
GDN_HEAD_DIM = 128
CONV_W = 4
MLA_HEADS = 8
KV_LORA = 256
NOPE_DIM = 128
ROPE_DIM = 64
V_HEAD_DIM = 128
PAGE_SIZE = 128

LANES = 128
SUBLANES = 8
VMEM_LIMIT_BYTES = 56 * 1024 * 1024
GDN_CHUNK_ROWS = 128
NEG_BIG = -1e30

bf16 = jnp.bfloat16
f32 = jnp.float32


def _cparams(*sem):
    return pltpu.CompilerParams(dimension_semantics=sem, vmem_limit_bytes=VMEM_LIMIT_BYTES)


def _rms(x, g):
    return x * lax.rsqrt(jnp.mean(x * x, axis=-1, keepdims=True) + EPS) * g


def _sigmoid(x):
    return 1.0 / (1.0 + jnp.exp(-x))


def _mm(a, b):
    return jnp.dot(a.astype(bf16), b.astype(bf16), preferred_element_type=f32)


def _mm_nt(a, b):
    return lax.dot_general(a.astype(bf16), b.astype(bf16), (((1,), (1,)), ((), ())),
                           preferred_element_type=f32)


def _mm_tn(a, b):
    return lax.dot_general(a.astype(bf16), b.astype(bf16), (((0,), (0,)), ((), ())),
                           preferred_element_type=f32)


def _row_tile(m, cap):
    t = min(m, cap)
    while m % t:
        t //= 2
    return t


def _ffn_kernel(x_ref, gpre_ref, wg_ref, wu_ref, wo_ref, gpost_ref, o_ref, h_scr, acc_scr):
    j = pl.program_id(1)

    @pl.when(j == 0)
    def _():
        h_scr[...] = _rms(x_ref[...], gpre_ref[...]).astype(bf16)
        acc_scr[...] = jnp.zeros_like(acc_scr)

    h = h_scr[...]
    g = jnp.dot(h, wg_ref[...], preferred_element_type=f32)
    u = jnp.dot(h, wu_ref[...], preferred_element_type=f32)
    a = (g * _sigmoid(g) * u).astype(bf16)
    acc_scr[...] += jnp.dot(a, wo_ref[...], preferred_element_type=f32)

    @pl.when(j == pl.num_programs(1) - 1)
    def _():
        o_ref[...] = x_ref[...] + 0.5 * _rms(acc_scr[...], gpost_ref[...])


def _ffn_chunk(f):
    for c in (256, 128):
        if f % c == 0:
            return c
    raise ValueError(f"FFN width {f} is not a multiple of {LANES}")


def ffn_sublayer(x, g_pre, w_in, w_out, g_post, layer, k):
    m, d = x.shape
    f = w_out.shape[2]
    tm = _row_tile(m, 1024)
    fc = _ffn_chunk(f)
    nf = f // fc
    return pl.pallas_call(
        _ffn_kernel,
        grid=(m // tm, nf),
        in_specs=[
            pl.BlockSpec((tm, d), lambda i, j: (i, 0)),
            pl.BlockSpec((1, d), lambda i, j: (0, 0)),
            pl.BlockSpec((None, None, d, fc), lambda i, j: (layer, k, 0, j)),
            pl.BlockSpec((None, None, d, fc), lambda i, j: (layer, k, 0, j + nf)),
            pl.BlockSpec((None, None, fc, d), lambda i, j: (layer, k, j, 0)),
            pl.BlockSpec((1, d), lambda i, j: (0, 0)),
        ],
        out_specs=pl.BlockSpec((tm, d), lambda i, j: (i, 0)),
        out_shape=jax.ShapeDtypeStruct((m, d), f32),
        scratch_shapes=[pltpu.VMEM((tm, d), bf16), pltpu.VMEM((tm, d), f32)],
        compiler_params=_cparams("parallel", "arbitrary"),
        name="ffn_sublayer",
    )(x, g_pre.reshape(1, d), w_in, w_in, w_out, g_post.reshape(1, d))


def _norm_proj_kernel(x_ref, g_ref, w_ref, o_ref, h_scr):
    @pl.when(pl.program_id(1) == 0)
    def _():
        h_scr[...] = _rms(x_ref[...], g_ref[...]).astype(bf16)

    o_ref[...] = jnp.dot(h_scr[...], w_ref[...], preferred_element_type=f32).astype(o_ref.dtype)


def norm_proj(x, g, w):
    m, d = x.shape
    n = w.shape[1]
    tm = _row_tile(m, 512)
    tn = n // 2 if (n // 2) % LANES == 0 and n > 2304 else n
    return pl.pallas_call(
        _norm_proj_kernel,
        grid=(m // tm, n // tn),
        in_specs=[
            pl.BlockSpec((tm, d), lambda i, j: (i, 0)),
            pl.BlockSpec((1, d), lambda i, j: (0, 0)),
            pl.BlockSpec((d, tn), lambda i, j: (0, j)),
        ],
        out_specs=pl.BlockSpec((tm, tn), lambda i, j: (i, j)),
        out_shape=jax.ShapeDtypeStruct((m, n), f32),
        scratch_shapes=[pltpu.VMEM((tm, d), bf16)],
        compiler_params=_cparams("parallel", "arbitrary"),
        name="gdn_in_proj",
    )(x, g.reshape(1, d), w)


def _out_proj_kernel(a_ref, w_ref, x_ref, g_ref, o_ref):
    y = jnp.dot(a_ref[...].astype(bf16), w_ref[...], preferred_element_type=f32)
    o_ref[...] = x_ref[...] + _rms(y, g_ref[...])


def out_proj_residual(a, w, x, g_post):
    m, d = x.shape
    k = a.shape[1]
    tm = _row_tile(m, 512)
    return pl.pallas_call(
        _out_proj_kernel,
        grid=(m // tm,),
        in_specs=[
            pl.BlockSpec((tm, k), lambda i: (i, 0)),
            pl.BlockSpec((k, d), lambda i: (0, 0)),
            pl.BlockSpec((tm, d), lambda i: (i, 0)),
            pl.BlockSpec((1, d), lambda i: (0, 0)),
        ],
        out_specs=pl.BlockSpec((tm, d), lambda i: (i, 0)),
        out_shape=jax.ShapeDtypeStruct((m, d), f32),
        compiler_params=_cparams("parallel"),
        name="out_proj_residual",
    )(a, w, x, g_post.reshape(1, d))


def _gdn_kernel(proj_ref, convw_ref, conv0_ref, s0_ref, dtb_ref, alog_ref, og_ref,
                o_ref, s_out_ref, s_scr, xp_scr, *, heads, rows):
    n = pl.program_id(1)
    hd = GDN_HEAD_DIM
    c = GDN_CHUNK_ROWS
    qk_dim = heads * hd
    conv_dim = 3 * qk_dim
    z_off = conv_dim
    b_off = z_off + qk_dim
    a_off = b_off + LANES
    pad = SUBLANES

    @pl.when(n == 0)
    def _():
        s_scr[...] = s0_ref[0]
        xp_scr[0:pad, :] = conv0_ref[0]
        if rows < c:
            xp_scr[pad + rows:pad + c, :] = jnp.zeros((c - rows, conv_dim), f32)

    if rows == c:
        @pl.when(n > 0)
        def _():
            xp_scr[0:pad, :] = xp_scr[c:c + pad, :]

    xp_scr[pad:pad + rows, :] = proj_ref[0, :, 0:conv_dim]

    def gate_block(off):
        blk = proj_ref[0, :, off:off + LANES]
        if rows < c:
            blk = jnp.concatenate([blk, jnp.zeros((c - rows, LANES), f32)], axis=0)
        return blk

    row_i = lax.broadcasted_iota(jnp.int32, (c, c), 0)
    col_i = lax.broadcasted_iota(jnp.int32, (c, c), 1)
    tril = row_i >= col_i
    strict = row_i > col_i
    real = lax.broadcasted_iota(jnp.int32, (c, LANES), 0) < rows

    beta = jnp.where(real, _sigmoid(gate_block(b_off)), 0.0)
    a_in = gate_block(a_off) + dtb_ref[...]
    softplus = jnp.maximum(a_in, 0.0) + jnp.log(1.0 + jnp.exp(-jnp.abs(a_in)))
    g = jnp.where(real, -jnp.exp(alog_ref[...]) * softplus, 0.0)
    g_hi = g.astype(bf16)
    g_mid = (g - g_hi.astype(f32)).astype(bf16)
    g_lo = (g - g_hi.astype(f32) - g_mid.astype(f32)).astype(bf16)
    gsum = jnp.dot(jnp.where(tril, 1.0, 0.0).astype(bf16), jnp.concatenate([g_hi, g_mid, g_lo], axis=1),
                   preferred_element_type=f32)
    gcum = gsum[:, :LANES] + gsum[:, LANES:2 * LANES] + gsum[:, 2 * LANES:]
    gcum_t = gcum.T
    glast = gcum[c - 1:c, :]
    e_g = jnp.exp(gcum)
    e_kd = jnp.exp(glast - gcum)
    e_last = jnp.exp(glast)

    def conv_silu(col0):
        acc = xp_scr[pad - 3:pad - 3 + c, col0:col0 + hd] * convw_ref[0:1, col0:col0 + hd]
        for j in range(1, CONV_W):
            acc = acc + xp_scr[pad - 3 + j:pad - 3 + j + c, col0:col0 + hd] * convw_ref[j:j + 1, col0:col0 + hd]
        return acc * _sigmoid(acc)

    def l2norm(t):
        return t * lax.rsqrt(jnp.sum(t * t, axis=-1, keepdims=True) + EPS)

    def col(t, h):
        return jnp.broadcast_to(t[:, h:h + 1], (t.shape[0], hd))

    n_iter = max(int(math.ceil(math.log2(rows))) - 1, 0)

    hs = range(heads)
    q = [l2norm(conv_silu(h * hd)) * (hd ** -0.5) for h in hs]
    k = [l2norm(conv_silu(qk_dim + h * hd)) for h in hs]
    v = [conv_silu(2 * qk_dim + h * hd) for h in hs]
    beta_c = [col(beta, h) for h in hs]
    e_g_c = [col(e_g, h) for h in hs]
    decay = []
    for h in hs:
        diff = col(gcum, h) - jnp.broadcast_to(gcum_t[h:h + 1, :], (c, c))
        decay.append(jnp.where(tril, jnp.exp(jnp.where(tril, diff, 0.0)), 0.0))
    kb = [k[h] * beta_c[h] for h in hs]
    k_bf = [k[h].astype(bf16) for h in hs]
    pw = [-jnp.where(strict, _mm_nt(kb[h], k_bf[h]) * decay[h], 0.0) for h in hs]
    nm = list(pw)
    for _ in range(n_iter):
        pw_bf = [p.astype(bf16) for p in pw]
        pw = [_mm(pw_bf[h], pw_bf[h]) for h in hs]
        pw_bf = [p.astype(bf16) for p in pw]
        nm = [nm[h] + pw[h] + _mm(nm[h], pw_bf[h]) for h in hs]
    rhs = [jnp.concatenate([v[h] * beta_c[h], kb[h] * e_g_c[h]], axis=1) for h in hs]
    sol = [rhs[h] + _mm(nm[h], rhs[h]) for h in hs]
    qk = [_mm_nt(q[h], k_bf[h]) * decay[h] for h in hs]
    qg = [q[h] * e_g_c[h] for h in hs]
    kdec = [k[h] * col(e_kd, h) for h in hs]
    s_prev = [s_scr[h] for h in hs]
    s_bf = [s.astype(bf16) for s in s_prev]
    v_new = [sol[h][:, :hd] - _mm(sol[h][:, hd:], s_bf[h]) for h in hs]
    v_bf = [t.astype(bf16) for t in v_new]
    o = [_mm(qg[h], s_bf[h]) + _mm(qk[h], v_bf[h]) for h in hs]
    for h in hs:
        s_scr[h] = s_prev[h] * e_last[:, h:h + 1] + _mm_tn(kdec[h], v_bf[h])
    for h in hs:
        zh = proj_ref[0, :, z_off + h * hd:z_off + (h + 1) * hd]
        o_ref[0, :, h * hd:(h + 1) * hd] = (_rms(o[h][:rows], og_ref[...]) * (zh * _sigmoid(zh))).astype(o_ref.dtype)

    @pl.when(n == pl.num_programs(1) - 1)
    def _():
        s_out_ref[0] = s_scr[...]


def gdn_core(proj, conv_w, conv0, s0, dt_bias, a_log, onorm_g, out_dtype):
    b, l, p = proj.shape
    heads = s0.shape[1]
    hd = GDN_HEAD_DIM
    conv_dim = 3 * heads * hd
    rows = min(l, GDN_CHUNK_ROWS)
    assert l % rows == 0 and (rows == GDN_CHUNK_ROWS or l == rows)
    conv0p = jnp.concatenate([jnp.zeros((b, SUBLANES - (CONV_W - 1), conv_dim), f32), conv0.astype(f32)], axis=1)
    pad_l = lambda t: jnp.pad(t.astype(f32), (0, LANES - heads)).reshape(1, LANES)
    kern = functools.partial(_gdn_kernel, heads=heads, rows=rows)
    return pl.pallas_call(
        kern,
        grid=(b, l // rows),
        in_specs=[
            pl.BlockSpec((1, rows, p), lambda i, n: (i, n, 0)),
            pl.BlockSpec((CONV_W, conv_dim), lambda i, n: (0, 0)),
            pl.BlockSpec((1, SUBLANES, conv_dim), lambda i, n: (i, 0, 0)),
            pl.BlockSpec((1, heads, hd, hd), lambda i, n: (i, 0, 0, 0)),
            pl.BlockSpec((1, LANES), lambda i, n: (0, 0)),
            pl.BlockSpec((1, LANES), lambda i, n: (0, 0)),
            pl.BlockSpec((1, hd), lambda i, n: (0, 0)),
        ],
        out_specs=[
            pl.BlockSpec((1, rows, heads * hd), lambda i, n: (i, n, 0)),
            pl.BlockSpec((1, heads, hd, hd), lambda i, n: (i, 0, 0, 0)),
        ],
        out_shape=[
            jax.ShapeDtypeStruct((b, l, heads * hd), out_dtype),
            jax.ShapeDtypeStruct(s0.shape, s0.dtype),
        ],
        scratch_shapes=[
            pltpu.VMEM((heads, hd, hd), f32),
            pltpu.VMEM((GDN_CHUNK_ROWS + SUBLANES, conv_dim), f32),
        ],
        compiler_params=_cparams("parallel", "arbitrary"),
        name="gdn_core",
    )(proj, conv_w, conv0p, s0, pad_l(dt_bias), pad_l(a_log), onorm_g.reshape(1, hd))


def _mla_kv_kernel(x_ref, gin_ref, wc_ref, wpa_ref, wpb_ref, gkv_ref, cos_ref, sin_ref,
                   c_ref, pe_ref, *attn_refs, tk):
    h = _rms(x_ref[...], gin_ref[...]).astype(bf16)
    ckv = _rms(jnp.dot(h, wc_ref[...], preferred_element_type=f32), gkv_ref[...])
    pa = jnp.dot(h, wpa_ref[...], preferred_element_type=f32)
    pb = jnp.dot(h, wpb_ref[...], preferred_element_type=f32)
    kpe = pa * cos_ref[...] + pb * sin_ref[...]
    c_ref[...] = ckv
    pe_ref[...] = kpe
    if attn_refs:
        k_ref, ct_ref = attn_refs
        k_ref[...] = jnp.concatenate([ckv, kpe], axis=1).astype(bf16)
        for s in range(ckv.shape[0] // tk):
            ct_ref[s] = ckv[s * tk:(s + 1) * tk].T.astype(bf16)


def _swap_halves(w):
    half = w.shape[-1] // 2
    return jnp.concatenate([w[..., half:], w[..., :half]], axis=-1)


def mla_shared_kv(x, g_in, w_dkv, g_kv, cos2, sin2, attn_tk=None):
    m, d = x.shape
    tm = _row_tile(cos2.shape[0], 512)
    nt = cos2.shape[0] // tm
    w_c = w_dkv[:, :KV_LORA].astype(bf16)
    w_pa = w_dkv[:, KV_LORA:].astype(bf16)
    w_pb = _swap_halves(w_pa)
    full = lambda r, c: pl.BlockSpec((r, c), lambda i: (0, 0))
    rows = lambda c: pl.BlockSpec((tm, c), lambda i: (i, 0))
    tab = pl.BlockSpec((tm, ROPE_DIM), lambda i: (i % nt, 0))
    out_specs = [rows(KV_LORA), rows(ROPE_DIM)]
    out_shape = [jax.ShapeDtypeStruct((m, KV_LORA), f32), jax.ShapeDtypeStruct((m, ROPE_DIM), f32)]
    if attn_tk is not None:
        assert tm % attn_tk == 0
        out_specs += [rows(KV_LORA + ROPE_DIM),
                      pl.BlockSpec((tm // attn_tk, KV_LORA, attn_tk), lambda i: (i, 0, 0))]
        out_shape += [jax.ShapeDtypeStruct((m, KV_LORA + ROPE_DIM), bf16),
                      jax.ShapeDtypeStruct((m // attn_tk, KV_LORA, attn_tk), bf16)]
    return pl.pallas_call(
        functools.partial(_mla_kv_kernel, tk=attn_tk),
        grid=(m // tm,),
        in_specs=[rows(d), full(1, d), full(d, KV_LORA), full(d, ROPE_DIM), full(d, ROPE_DIM),
                  full(1, KV_LORA), tab, tab],
        out_specs=out_specs,
        out_shape=out_shape,
        compiler_params=_cparams("parallel"),
        name="mla_shared_kv",
    )(x, g_in.reshape(1, d), w_c, w_pa, w_pb, g_kv.reshape(1, KV_LORA), cos2, sin2)


def _mla_q_kernel(x_ref, gpre_ref, wdq_ref, gq_ref, wn_ref, wpa_ref, wpb_ref, wuk_ref, cos_ref, sin_ref,
                  q_ref, *, heads, scale):
    h = _rms(x_ref[...], gpre_ref[...]).astype(bf16)
    qa = _rms(jnp.dot(h, wdq_ref[...], preferred_element_type=f32), gq_ref[...]).astype(bf16)
    nope = jnp.dot(qa, wn_ref[...], preferred_element_type=f32).astype(bf16)
    pa = jnp.dot(qa, wpa_ref[...], preferred_element_type=f32)
    pb = jnp.dot(qa, wpb_ref[...], preferred_element_type=f32)
    qpe = (pa * cos_ref[...] + pb * sin_ref[...]) * scale
    for hh in range(heads):
        qlat = jnp.dot(nope[:, hh * NOPE_DIM:(hh + 1) * NOPE_DIM], wuk_ref[hh], preferred_element_type=f32)
        q_ref[hh] = jnp.concatenate([qlat * scale, qpe[:, hh * ROPE_DIM:(hh + 1) * ROPE_DIM]],
                                    axis=1).astype(q_ref.dtype)


def mla_query(x, g_pre, w_dq, g_q, w_uq, w_uk, cos2h, sin2h, out_dtype):
    m, d = x.shape
    heads = MLA_HEADS
    q_lora = w_dq.shape[1]
    qk_head = NOPE_DIM + ROPE_DIM
    tm = _row_tile(cos2h.shape[0], 512)
    nt = cos2h.shape[0] // tm
    w3 = w_uq.reshape(q_lora, heads, qk_head)
    w_n = w3[:, :, :NOPE_DIM].reshape(q_lora, heads * NOPE_DIM).astype(bf16)
    w_pa3 = w3[:, :, NOPE_DIM:]
    w_pa = w_pa3.reshape(q_lora, heads * ROPE_DIM).astype(bf16)
    w_pb = _swap_halves(w_pa3).reshape(q_lora, heads * ROPE_DIM).astype(bf16)
    w_ukt = jnp.transpose(w_uk.reshape(KV_LORA, heads, NOPE_DIM), (1, 2, 0)).astype(bf16)
    scale = qk_head ** -0.5
    full = lambda *s: pl.BlockSpec(s, lambda i: (0,) * len(s))
    tab = pl.BlockSpec((tm, heads * ROPE_DIM), lambda i: (i % nt, 0))
    kern = functools.partial(_mla_q_kernel, heads=heads, scale=scale)
    return pl.pallas_call(
        kern,
        grid=(m // tm,),
        in_specs=[pl.BlockSpec((tm, d), lambda i: (i, 0)), full(1, d), full(d, q_lora), full(1, q_lora),
                  full(q_lora, heads * NOPE_DIM), full(q_lora, heads * ROPE_DIM), full(q_lora, heads * ROPE_DIM),
                  full(heads, NOPE_DIM, KV_LORA), tab, tab],
        out_specs=pl.BlockSpec((heads, tm, KV_LORA + ROPE_DIM), lambda i: (0, i, 0)),
        out_shape=jax.ShapeDtypeStruct((heads, m, KV_LORA + ROPE_DIM), out_dtype),
        compiler_params=_cparams("parallel"),
        name="mla_query",
    )(x, g_pre.reshape(1, d), w_dq.astype(bf16), g_q.reshape(1, q_lora), w_n, w_pa, w_pb, w_ukt, cos2h, sin2h)


def _softmax_update(s, pv_fn, m_scr, l_scr, acc_scr):
    m_prev = m_scr[...]
    m_new = jnp.maximum(m_prev, jnp.max(s, axis=-1, keepdims=True))
    alpha = jnp.exp(m_prev - m_new)
    p = jnp.exp(s - m_new[:, 0:1])
    l_scr[...] = alpha * l_scr[...] + jnp.sum(p, axis=-1, keepdims=True)
    acc_scr[...] = acc_scr[...] * alpha[:, 0:1] + pv_fn(p.astype(bf16))
    m_scr[...] = m_new


def _softmax_init(m_scr, l_scr, acc_scr):
    m_scr[...] = jnp.full_like(m_scr, NEG_BIG)
    l_scr[...] = jnp.zeros_like(l_scr)
    acc_scr[...] = jnp.zeros_like(acc_scr)


PROMPT_HEAD_GROUP = 8


def _prompt_attn_kernel(q_ref, k_ref, kt_ref, wuvt_ref, o_ref, acc_scr, *, heads, qb):
    i = pl.program_id(1)
    tk = qb
    grp = PROMPT_HEAD_GROUP
    k_idx = lax.broadcasted_iota(jnp.int32, (tk, qb), 0)
    q_idx = lax.broadcasted_iota(jnp.int32, (tk, qb), 1)
    causal = k_idx <= q_idx

    for h0 in range(0, heads, grp):
        qs = [q_ref[h0 + g] for g in range(grp)]

        def block(j, carry, masked):
            kj = k_ref[pl.ds(pl.multiple_of(j * tk, tk), tk), :]
            ktj = kt_ref[j]
            sts = [_mm_nt(kj, qs[g]) for g in range(grp)]
            if masked:
                sts = [jnp.where(causal, st, NEG_BIG) for st in sts]
            new = []
            for g in range(grp):
                m_prev, l_prev = carry[g]
                m_new = jnp.maximum(m_prev, jnp.max(sts[g], axis=0, keepdims=True))
                alpha = jnp.exp(m_prev - m_new)
                p = jnp.exp(sts[g] - m_new)
                l_new = alpha * l_prev + jnp.sum(p, axis=0, keepdims=True)
                acc_scr[g] = acc_scr[g] * alpha + jnp.dot(ktj, p.astype(bf16), preferred_element_type=f32)
                new.append((m_new, l_new))
            return tuple(new)

        acc_scr[...] = jnp.zeros_like(acc_scr)
        init = tuple((jnp.full((1, qb), NEG_BIG, f32), jnp.zeros((1, qb), f32)) for _ in range(grp))
        carry = lax.fori_loop(0, i, lambda j, c: block(j, c, False), init)
        carry = block(i, carry, True)
        for g in range(grp):
            o_t = (acc_scr[g] / carry[g][1]).astype(bf16)
            head_t = jnp.dot(wuvt_ref[h0 + g], o_t, preferred_element_type=f32)
            o_ref[:, (h0 + g) * V_HEAD_DIM:(h0 + g + 1) * V_HEAD_DIM] = head_t.T.astype(o_ref.dtype)


def prompt_attention(q, keys, kt, w_uvt, batch, seq):
    heads = q.shape[0]
    qk = q.shape[2]
    qb = kt.shape[2]
    nq = seq // qb
    kern = functools.partial(_prompt_attn_kernel, heads=heads, qb=qb)
    return pl.pallas_call(
        kern,
        grid=(batch, nq),
        in_specs=[
            pl.BlockSpec((heads, qb, qk), lambda b, i: (0, b * nq + i, 0)),
            pl.BlockSpec((seq, qk), lambda b, i: (b, 0)),
            pl.BlockSpec((nq, KV_LORA, qb), lambda b, i: (b, 0, 0)),
            pl.BlockSpec((heads, V_HEAD_DIM, KV_LORA), lambda b, i: (0, 0, 0)),
        ],
        out_specs=pl.BlockSpec((qb, heads * V_HEAD_DIM), lambda b, i: (b * nq + i, 0)),
        out_shape=jax.ShapeDtypeStruct((batch * seq, heads * V_HEAD_DIM), bf16),
        scratch_shapes=[pltpu.VMEM((PROMPT_HEAD_GROUP, KV_LORA, qb), f32)],
        compiler_params=_cparams("parallel", "arbitrary"),
        name="prompt_attention",
    )(q, keys, kt, w_uvt)


def _decode_attn_kernel(pt_ref, q_ref, cnew_ref, penew_ref, wuv_ref, *rest, heads, seq, pages):
    ck_refs = rest[:pages]
    pe_refs = rest[pages:2 * pages]
    o_ref, m_scr, l_scr, acc_scr = rest[2 * pages:]
    j = pl.program_id(1)
    last = pl.num_programs(1) - 1
    r = heads * seq
    q_all = q_ref[...].reshape(r, KV_LORA + ROPE_DIM).astype(bf16)
    q = q_all[:, :KV_LORA]
    qp = q_all[:, KV_LORA:]

    @pl.when(j == 0)
    def _():
        _softmax_init(m_scr, l_scr, acc_scr)

    @pl.when(j < last)
    def _():
        kcs = [ck_refs[p][0].astype(bf16) for p in range(pages)]
        s = jnp.concatenate(
            [_mm_nt(q, kcs[p]) + _mm(qp, pe_refs[p][0]) for p in range(pages)], axis=1)

        def pv(pm):
            acc = jnp.dot(pm[:, 0:PAGE_SIZE], kcs[0], preferred_element_type=f32)
            for p in range(1, pages):
                acc = acc + jnp.dot(pm[:, p * PAGE_SIZE:(p + 1) * PAGE_SIZE], kcs[p], preferred_element_type=f32)
            return acc

        _softmax_update(s, pv, m_scr, l_scr, acc_scr)

    @pl.when(j == last)
    def _():
        kc = jnp.concatenate([cnew_ref[0], jnp.zeros((PAGE_SIZE - seq, KV_LORA), f32)], axis=0).astype(bf16)
        kp = jnp.concatenate([penew_ref[0], jnp.zeros((PAGE_SIZE - seq, ROPE_DIM), f32)], axis=0)
        s = _mm_nt(q, kc) + _mm_nt(qp, kp)
        q_t = lax.rem(lax.broadcasted_iota(jnp.int32, (r, PAGE_SIZE), 0), jnp.int32(seq))
        k_t = lax.broadcasted_iota(jnp.int32, (r, PAGE_SIZE), 1)
        s = jnp.where(k_t <= q_t, s, NEG_BIG)
        _softmax_update(s, lambda pm: jnp.dot(pm, kc, preferred_element_type=f32), m_scr, l_scr, acc_scr)
        o_lat = (acc_scr[...] / l_scr[:, 0:1]).astype(bf16)
        for hh in range(heads):
            full = jnp.dot(o_lat, wuv_ref[hh], preferred_element_type=f32)
            o_ref[:, hh * V_HEAD_DIM:(hh + 1) * V_HEAD_DIM] = full[hh * seq:(hh + 1) * seq]


def decode_attention(q, c_new, pe_new, cache_ckv, cache_kpe_t, page_table, w_uv3):
    heads = q.shape[0]
    db, seq, _ = c_new.shape
    n_pages = page_table.shape[1]
    pages = 8 if n_pages % 8 == 0 and n_pages >= 16 else (4 if n_pages % 4 == 0 else 1)
    ns = n_pages // pages
    r = heads * seq

    def page_spec(p, shape):
        return pl.BlockSpec((1,) + shape, lambda b, j, pt: (pt[b, jnp.minimum(j, ns - 1) * pages + p], 0, 0))

    kern = functools.partial(_decode_attn_kernel, heads=heads, seq=seq, pages=pages)
    grid_spec = pltpu.PrefetchScalarGridSpec(
        num_scalar_prefetch=1,
        grid=(db, ns + 1),
        in_specs=[
            pl.BlockSpec((heads, seq, KV_LORA + ROPE_DIM), lambda b, j, pt: (0, b, 0)),
            pl.BlockSpec((1, seq, KV_LORA), lambda b, j, pt: (b, 0, 0)),
            pl.BlockSpec((1, seq, ROPE_DIM), lambda b, j, pt: (b, 0, 0)),
            pl.BlockSpec((heads, KV_LORA, V_HEAD_DIM), lambda b, j, pt: (0, 0, 0)),
        ] + [page_spec(p, (PAGE_SIZE, KV_LORA)) for p in range(pages)]
          + [page_spec(p, (ROPE_DIM, PAGE_SIZE)) for p in range(pages)],
        out_specs=pl.BlockSpec((seq, heads * V_HEAD_DIM), lambda b, j, pt: (b, 0)),
        scratch_shapes=[pltpu.VMEM((r, LANES), f32), pltpu.VMEM((r, LANES), f32), pltpu.VMEM((r, KV_LORA), f32)],
    )
    return pl.pallas_call(
        kern,
        grid_spec=grid_spec,
        out_shape=jax.ShapeDtypeStruct((db * seq, heads * V_HEAD_DIM), f32),
        compiler_params=_cparams("parallel", "arbitrary"),
        name="decode_attention",
    )(page_table, q, c_new, pe_new, w_uv3, *([cache_ckv] * pages), *([cache_kpe_t] * pages))


def _rope_tables(first_pos, seq):
    half = ROPE_DIM // 2
    inv_freq = ROPE_THETA ** (-jnp.arange(half, dtype=f32) / half)
    pos = first_pos + jnp.arange(seq, dtype=jnp.int32)
    ang = pos.astype(f32)[:, None] * inv_freq[None, :]
    cos, sin = jnp.cos(ang), jnp.sin(ang)
    return jnp.concatenate([cos, cos], axis=1), jnp.concatenate([-sin, sin], axis=1)


def _run_group(x3, gdn_s0, conv0, past, wts):
    batch, seq, d = x3.shape
    n_a = wts["gdn_w_in"].shape[0]
    depth = wts["ffn_in_b"].shape[0]
    x = x3.reshape(batch * seq, d)
    past_len = 0 if past is None else past[2].shape[1] * PAGE_SIZE
    cos2, sin2 = _rope_tables(past_len, seq)
    if seq < LANES:
        cos2, sin2 = jnp.tile(cos2, (batch, 1)), jnp.tile(sin2, (batch, 1))
    cos2h, sin2h = jnp.tile(cos2, (1, MLA_HEADS)), jnp.tile(sin2, (1, MLA_HEADS))
    act_dtype = bf16 if past is None else f32
    states, convs = [], []
    c_new = pe_new = keys_bf = kt_bf = None
    attn_tk = _row_tile(seq, 256) if past is None else None
    for layer in range(depth):
        if layer == n_a:
            kv = mla_shared_kv(x, wts["mla_kvin_norm"], wts["mla_w_dkv"], wts["mla_kv_norm"], cos2, sin2, attn_tk)
            c_new, pe_new = kv[0], kv[1]
            if past is None:
                keys_bf, kt_bf = kv[2], kv[3]
        x = ffn_sublayer(x, wts["norm_pre"][layer, 0], wts["ffn_in_b"], wts["ffn_out_b"],
                         wts["norm_post"][layer, 0], layer, 0)
        if layer < n_a:
            proj = norm_proj(x, wts["norm_pre"][layer, 1], wts["gdn_w_in_b"][layer])
            proj3 = proj.reshape(batch, seq, -1)
            conv_dim = conv0.shape[-1]
            o, s_new = gdn_core(proj3, wts["gdn_conv_w"][layer], conv0[layer], gdn_s0[layer],
                                wts["gdn_dt_bias"][layer], wts["gdn_a_log"][layer], wts["gdn_norm"][layer], act_dtype)
            states.append(s_new)
            if seq >= CONV_W - 1:
                convs.append(proj3[:, seq - (CONV_W - 1):, :conv_dim])
            else:
                prev = jnp.concatenate([conv0[layer].astype(f32), proj3[:, :, :conv_dim]], axis=1)
                convs.append(prev[:, -(CONV_W - 1):])
            x = out_proj_residual(o.reshape(batch * seq, -1), wts["gdn_w_out_b"][layer], x, wts["norm_post"][layer, 1])
        else:
            jb = layer - n_a
            q = mla_query(x, wts["norm_pre"][layer, 1], wts["mla_w_dq"][jb], wts["mla_q_norm"][jb],
                                    wts["mla_w_uq"][jb], wts["mla_w_uk"], cos2h, sin2h, act_dtype)
            if past is None:
                o = prompt_attention(q, keys_bf, kt_bf, wts["w_uvt"], batch, seq)
            else:
                o = decode_attention(q, c_new.reshape(batch, seq, -1), pe_new.reshape(batch, seq, -1),
                                     past[0], past[1], past[2], wts["w_uv3"])
            x = out_proj_residual(o, wts["mla_w_o_b"][jb], x, wts["norm_post"][layer, 1])
        x = ffn_sublayer(x, wts["norm_pre"][layer, 2], wts["ffn_in_b"], wts["ffn_out_b"],
                         wts["norm_post"][layer, 2], layer, 1)
    return (x.reshape(batch, seq, d), jnp.stack(states), jnp.stack(convs),
            c_new.reshape(batch, seq, -1), pe_new.reshape(batch, seq, -1))


def _pack_gdn_w_in(w_in, heads):
    main = 4 * heads * GDN_HEAD_DIM
    padw = ((0, 0), (0, 0), (0, LANES - heads))
    b = jnp.pad(w_in[..., main:main + heads], padw)
    a = jnp.pad(w_in[..., main + heads:main + 2 * heads], padw)
    return jnp.concatenate([w_in[..., :main], b, a], axis=-1).astype(bf16)


def kernel(x_prompt, x_sample, state_gdn, state_conv, cache_ckv, cache_kpe, page_table, norm_pre, norm_post, ffn_in, ffn_out, gdn_w_in, gdn_conv_w, gdn_dt_bias, gdn_a_log, gdn_norm, gdn_w_out, mla_kvin_norm, mla_w_dkv, mla_kv_norm, mla_w_uk, mla_w_uv, mla_w_dq, mla_q_norm, mla_w_uq, mla_w_o):
    n_a = gdn_w_in.shape[0]
    b = x_prompt.shape[0]
    heads = state_gdn.shape[2]
    wts = dict(
        norm_pre=norm_pre, norm_post=norm_post,
        ffn_in_b=ffn_in.astype(bf16), ffn_out_b=ffn_out.astype(bf16),
        gdn_w_in=gdn_w_in, gdn_w_in_b=_pack_gdn_w_in(gdn_w_in, heads), gdn_conv_w=gdn_conv_w,
        gdn_dt_bias=gdn_dt_bias, gdn_a_log=gdn_a_log, gdn_norm=gdn_norm, gdn_w_out_b=gdn_w_out.astype(bf16),
        mla_kvin_norm=mla_kvin_norm, mla_w_dkv=mla_w_dkv, mla_kv_norm=mla_kv_norm, mla_w_uk=mla_w_uk,
        w_uv3=jnp.transpose(mla_w_uv.reshape(KV_LORA, MLA_HEADS, V_HEAD_DIM), (1, 0, 2)).astype(bf16),
        w_uvt=jnp.transpose(mla_w_uv.reshape(KV_LORA, MLA_HEADS, V_HEAD_DIM), (1, 2, 0)).astype(bf16),
        mla_w_dq=mla_w_dq, mla_q_norm=mla_q_norm, mla_w_uq=mla_w_uq, mla_w_o_b=mla_w_o.astype(bf16),
    )
    zero_s = jnp.zeros((n_a, b) + state_gdn.shape[2:], state_gdn.dtype)
    zero_conv = jnp.zeros((n_a, b) + state_conv.shape[2:], state_conv.dtype)
    y_p, sg_p, cv_p, ckv_p, kpe_p = _run_group(x_prompt, zero_s, zero_conv, None, wts)
    y_s, sg_s, cv_s, ckv_s, kpe_s = _run_group(x_sample, state_gdn, state_conv,
                                               (cache_ckv, jnp.swapaxes(cache_kpe, 1, 2), page_table), wts)
    return (y_p, y_s, sg_p, cv_p, ckv_p, kpe_p, sg_s, cv_s, ckv_s, kpe_s)
```

```python
import functools
import math

import jax
import jax.numpy as jnp
from jax import lax
from jax.experimental import pallas as pl
from jax.experimental.pallas import tpu as pltpu

EPS = 1e-6
ROPE_THETA = 10000.0
GDN_HEADS = 8
GDN_HEAD_DIM = 128
CONV_W = 4
MLA_HEADS = 8
KV_LORA = 256
NOPE_DIM = 128
ROPE_DIM = 64
V_HEAD_DIM = 128
PAGE_SIZE = 128

LANES = 128
SUBLANES = 8
VMEM_LIMIT_BYTES = 56 * 1024 * 1024
GDN_CHUNK_ROWS = 128
NEG_BIG = -1e30

bf16 = jnp.bfloat16
f32 = jnp.float32


def _cparams(*sem):
    return pltpu.CompilerParams(dimension_semantics=sem, vmem_limit_bytes=VMEM_LIMIT_BYTES)


def _rms(x, g):
    return x * lax.rsqrt(jnp.mean(x * x, axis=-1, keepdims=True) + EPS) * g


def _sigmoid(x):
    return 1.0 / (1.0 + jnp.exp(-x))


def _mm(a, b):
    return jnp.dot(a.astype(bf16), b.astype(bf16), preferred_element_type=f32)


def _mm_nt(a, b):
    return lax.dot_general(a.astype(bf16), b.astype(bf16), (((1,), (1,)), ((), ())),
                           preferred_element_type=f32)


def _mm_tn(a, b):
    return lax.dot_general(a.astype(bf16), b.astype(bf16), (((0,), (0,)), ((), ())),
                           preferred_element_type=f32)


def _row_tile(m, cap):
    t = min(m, cap)
    while m % t:
        t //= 2
    return t


def _ffn_kernel(x_ref, gpre_ref, wg_ref, wu_ref, wo_ref, gpost_ref, o_ref, h_scr, acc_scr):
    j = pl.program_id(1)

    @pl.when(j == 0)
    def _():
        h_scr[...] = _rms(x_ref[...], gpre_ref[...]).astype(bf16)
        acc_scr[...] = jnp.zeros_like(acc_scr)

    h = h_scr[...]
    g = jnp.dot(h, wg_ref[...], preferred_element_type=f32)
    u = jnp.dot(h, wu_ref[...], preferred_element_type=f32)
    a = (g * _sigmoid(g) * u).astype(bf16)
    acc_scr[...] += jnp.dot(a, wo_ref[...], preferred_element_type=f32)

    @pl.when(j == pl.num_programs(1) - 1)
    def _():
        o_ref[...] = x_ref[...] + 0.5 * _rms(acc_scr[...], gpost_ref[...])


def _ffn_chunk(f):
    for c in (256, 128):
        if f % c == 0:
            return c
    raise ValueError(f"FFN width {f} is not a multiple of {LANES}")


def ffn_sublayer(x, g_pre, w_in, w_out, g_post, layer, k):
    m, d = x.shape
    f = w_out.shape[2]
    tm = _row_tile(m, 1024)
    fc = _ffn_chunk(f)
    nf = f // fc
    return pl.pallas_call(
        _ffn_kernel,
        grid=(m // tm, nf),
        in_specs=[
            pl.BlockSpec((tm, d), lambda i, j: (i, 0)),
            pl.BlockSpec((1, d), lambda i, j: (0, 0)),
            pl.BlockSpec((None, None, d, fc), lambda i, j: (layer, k, 0, j)),
            pl.BlockSpec((None, None, d, fc), lambda i, j: (layer, k, 0, j + nf)),
            pl.BlockSpec((None, None, fc, d), lambda i, j: (layer, k, j, 0)),
            pl.BlockSpec((1, d), lambda i, j: (0, 0)),
        ],
        out_specs=pl.BlockSpec((tm, d), lambda i, j: (i, 0)),
        out_shape=jax.ShapeDtypeStruct((m, d), f32),
        scratch_shapes=[pltpu.VMEM((tm, d), bf16), pltpu.VMEM((tm, d), f32)],
        compiler_params=_cparams("parallel", "arbitrary"),
        name="ffn_sublayer",
    )(x, g_pre.reshape(1, d), w_in, w_in, w_out, g_post.reshape(1, d))


def _norm_proj_kernel(x_ref, g_ref, w_ref, o_ref, h_scr):
    @pl.when(pl.program_id(1) == 0)
    def _():
        h_scr[...] = _rms(x_ref[...], g_ref[...]).astype(bf16)

    o_ref[...] = jnp.dot(h_scr[...], w_ref[...], preferred_element_type=f32).astype(o_ref.dtype)


def norm_proj(x, g, w, out_dtype):
    m, d = x.shape
    n = w.shape[1]
    tm = _row_tile(m, 1024)
    tn = n // 2 if (n // 2) % LANES == 0 and n > 2304 else n
    return pl.pallas_call(
        _norm_proj_kernel,
        grid=(m // tm, n // tn),
        in_specs=[
            pl.BlockSpec((tm, d), lambda i, j: (i, 0)),
            pl.BlockSpec((1, d), lambda i, j: (0, 0)),
            pl.BlockSpec((d, tn), lambda i, j: (0, j)),
        ],
        out_specs=pl.BlockSpec((tm, tn), lambda i, j: (i, j)),
        out_shape=jax.ShapeDtypeStruct((m, n), out_dtype),
        scratch_shapes=[pltpu.VMEM((tm, d), bf16)],
        compiler_params=_cparams("parallel", "arbitrary"),
        name="gdn_in_proj",
    )(x, g.reshape(1, d), w)


def _out_proj_kernel(a_ref, w_ref, x_ref, g_ref, o_ref):
    y = jnp.dot(a_ref[...].astype(bf16), w_ref[...], preferred_element_type=f32)
    o_ref[...] = x_ref[...] + _rms(y, g_ref[...])


def out_proj_residual(a, w, x, g_post):
    m, d = x.shape
    k = a.shape[1]
    tm = _row_tile(m, 512)
    return pl.pallas_call(
        _out_proj_kernel,
        grid=(m // tm,),
        in_specs=[
            pl.BlockSpec((tm, k), lambda i: (i, 0)),
            pl.BlockSpec((k, d), lambda i: (0, 0)),
            pl.BlockSpec((tm, d), lambda i: (i, 0)),
            pl.BlockSpec((1, d), lambda i: (0, 0)),
        ],
        out_specs=pl.BlockSpec((tm, d), lambda i: (i, 0)),
        out_shape=jax.ShapeDtypeStruct((m, d), f32),
        compiler_params=_cparams("parallel"),
        name="out_proj_residual",
    )(a, w, x, g_post.reshape(1, d))


def _gdn_kernel(proj_ref, convw_ref, conv0_ref, s0_ref, dtb_ref, alog_ref, og_ref,
                o_ref, s_out_ref, s_scr, xp_scr, *, heads, rows):
    n = pl.program_id(1)
    hd = GDN_HEAD_DIM
    c = GDN_CHUNK_ROWS
    qk_dim = heads * hd
    conv_dim = 3 * qk_dim
    z_off = conv_dim
    b_off = z_off + qk_dim
    a_off = b_off + LANES
    pad = SUBLANES

    @pl.when(n == 0)
    def _():
        s_scr[...] = s0_ref[0]
        xp_scr[0:pad, :] = conv0_ref[0]
        if rows < c:
            xp_scr[pad + rows:pad + c, :] = jnp.zeros((c - rows, conv_dim), f32)

    if rows == c:
        @pl.when(n > 0)
        def _():
            xp_scr[0:pad, :] = xp_scr[c:c + pad, :]

    xp_scr[pad:pad + rows, :] = proj_ref[0, :, 0:conv_dim].astype(f32)

    def gate_block(off):
        blk = proj_ref[0, :, off:off + LANES].astype(f32)
        if rows < c:
            blk = jnp.concatenate([blk, jnp.zeros((c - rows, LANES), f32)], axis=0)
        return blk

    row_i = lax.broadcasted_iota(jnp.int32, (c, c), 0)
    col_i = lax.broadcasted_iota(jnp.int32, (c, c), 1)
    tril = row_i >= col_i
    strict = row_i > col_i
    real = lax.broadcasted_iota(jnp.int32, (c, LANES), 0) < rows

    beta = jnp.where(real, _sigmoid(gate_block(b_off)), 0.0)
    a_in = gate_block(a_off) + dtb_ref[...]
    softplus = jnp.maximum(a_in, 0.0) + jnp.log(1.0 + jnp.exp(-jnp.abs(a_in)))
    g = jnp.where(real, -jnp.exp(alog_ref[...]) * softplus, 0.0)
    g_hi = g.astype(bf16)
    g_mid = (g - g_hi.astype(f32)).astype(bf16)
    g_lo = (g - g_hi.astype(f32) - g_mid.astype(f32)).astype(bf16)
    gsum = jnp.dot(jnp.where(tril, 1.0, 0.0).astype(bf16), jnp.concatenate([g_hi, g_mid, g_lo], axis=1),
                   preferred_element_type=f32)
    gcum = gsum[:, :LANES] + gsum[:, LANES:2 * LANES] + gsum[:, 2 * LANES:]
    gcum_t = gcum.T
    glast = gcum[c - 1:c, :]
    e_g = jnp.exp(gcum)
    e_kd = jnp.exp(glast - gcum)
    e_last = jnp.exp(glast)

    def conv_silu(col0):
        acc = xp_scr[pad - 3:pad - 3 + c, col0:col0 + hd] * convw_ref[0:1, col0:col0 + hd]
        for j in range(1, CONV_W):
            acc = acc + xp_scr[pad - 3 + j:pad - 3 + j + c, col0:col0 + hd] * convw_ref[j:j + 1, col0:col0 + hd]
        return acc * _sigmoid(acc)

    def l2norm(t):
        return t * lax.rsqrt(jnp.sum(t * t, axis=-1, keepdims=True) + EPS)

    def col(t, h):
        return jnp.broadcast_to(t[:, h:h + 1], (t.shape[0], hd))

    n_iter = max(int(math.ceil(math.log2(rows))) - 1, 0)

    hs = range(heads)
    q = [l2norm(conv_silu(h * hd)) * (hd ** -0.5) for h in hs]
    k = [l2norm(conv_silu(qk_dim + h * hd)) for h in hs]
    v = [conv_silu(2 * qk_dim + h * hd) for h in hs]
    beta_c = [col(beta, h) for h in hs]
    e_g_c = [col(e_g, h) for h in hs]
    decay = []
    for h in hs:
        diff = col(gcum, h) - jnp.broadcast_to(gcum_t[h:h + 1, :], (c, c))
        decay.append(jnp.where(tril, jnp.exp(jnp.where(tril, diff, 0.0)), 0.0))
    kb = [k[h] * beta_c[h] for h in hs]
    k_bf = [k[h].astype(bf16) for h in hs]
    pw = [-jnp.where(strict, _mm_nt(kb[h], k_bf[h]) * decay[h], 0.0) for h in hs]
    nm = list(pw)
    for _ in range(n_iter):
        pw_bf = [p.astype(bf16) for p in pw]
        pw = [_mm(pw_bf[h], pw_bf[h]) for h in hs]
        pw_bf = [p.astype(bf16) for p in pw]
        nm = [nm[h] + pw[h] + _mm(nm[h], pw_bf[h]) for h in hs]
    rhs = [jnp.concatenate([v[h] * beta_c[h], kb[h] * e_g_c[h]], axis=1) for h in hs]
    sol = [rhs[h] + _mm(nm[h], rhs[h]) for h in hs]
    qk = [_mm_nt(q[h], k_bf[h]) * decay[h] for h in hs]
    qg = [q[h] * e_g_c[h] for h in hs]
    kdec = [k[h] * col(e_kd, h) for h in hs]
    s_prev = [s_scr[h] for h in hs]
    s_bf = [s.astype(bf16) for s in s_prev]
    v_new = [sol[h][:, :hd] - _mm(sol[h][:, hd:], s_bf[h]) for h in hs]
    v_bf = [t.astype(bf16) for t in v_new]
    o = [_mm(qg[h], s_bf[h]) + _mm(qk[h], v_bf[h]) for h in hs]
    for h in hs:
        s_scr[h] = s_prev[h] * e_last[:, h:h + 1] + _mm_tn(kdec[h], v_bf[h])
    for h in hs:
        zh = proj_ref[0, :, z_off + h * hd:z_off + (h + 1) * hd].astype(f32)
        o_ref[0, :, h * hd:(h + 1) * hd] = (_rms(o[h][:rows], og_ref[...]) * (zh * _sigmoid(zh))).astype(o_ref.dtype)

    @pl.when(n == pl.num_programs(1) - 1)
    def _():
        s_out_ref[0] = s_scr[...]


def gdn_core(proj, conv_w, conv0, s0, dt_bias, a_log, onorm_g, out_dtype):
    b, l, p = proj.shape
    heads = s0.shape[1]
    hd = GDN_HEAD_DIM
    conv_dim = 3 * heads * hd
    rows = min(l, GDN_CHUNK_ROWS)
    assert l % rows == 0 and (rows == GDN_CHUNK_ROWS or l == rows)
    conv0p = jnp.concatenate([jnp.zeros((b, SUBLANES - (CONV_W - 1), conv_dim), f32), conv0.astype(f32)], axis=1)
    pad_l = lambda t: jnp.pad(t.astype(f32), (0, LANES - heads)).reshape(1, LANES)
    kern = functools.partial(_gdn_kernel, heads=heads, rows=rows)
    return pl.pallas_call(
        kern,
        grid=(b, l // rows),
        in_specs=[
            pl.BlockSpec((1, rows, p), lambda i, n: (i, n, 0)),
            pl.BlockSpec((CONV_W, conv_dim), lambda i, n: (0, 0)),
            pl.BlockSpec((1, SUBLANES, conv_dim), lambda i, n: (i, 0, 0)),
            pl.BlockSpec((1, heads, hd, hd), lambda i, n: (i, 0, 0, 0)),
            pl.BlockSpec((1, LANES), lambda i, n: (0, 0)),
            pl.BlockSpec((1, LANES), lambda i, n: (0, 0)),
            pl.BlockSpec((1, hd), lambda i, n: (0, 0)),
        ],
        out_specs=[
            pl.BlockSpec((1, rows, heads * hd), lambda i, n: (i, n, 0)),
            pl.BlockSpec((1, heads, hd, hd), lambda i, n: (i, 0, 0, 0)),
        ],
        out_shape=[
            jax.ShapeDtypeStruct((b, l, heads * hd), out_dtype),
            jax.ShapeDtypeStruct(s0.shape, s0.dtype),
        ],
        scratch_shapes=[
            pltpu.VMEM((heads, hd, hd), f32),
            pltpu.VMEM((GDN_CHUNK_ROWS + SUBLANES, conv_dim), f32),
        ],
        compiler_params=_cparams("parallel", "arbitrary"),
        name="gdn_core",
    )(proj, conv_w, conv0p, s0, pad_l(dt_bias), pad_l(a_log), onorm_g.reshape(1, hd))


def _mla_kv_kernel(x_ref, gin_ref, wc_ref, wpa_ref, wpb_ref, gkv_ref, cos_ref, sin_ref,
                   c_ref, pe_ref, *attn_refs, tk):
    h = _rms(x_ref[...], gin_ref[...]).astype(bf16)
    ckv = _rms(jnp.dot(h, wc_ref[...], preferred_element_type=f32), gkv_ref[...])
    pa = jnp.dot(h, wpa_ref[...], preferred_element_type=f32)
    pb = jnp.dot(h, wpb_ref[...], preferred_element_type=f32)
    kpe = pa * cos_ref[...] + pb * sin_ref[...]
    c_ref[...] = ckv
    pe_ref[...] = kpe
    if attn_refs:
        k_ref, ct_ref = attn_refs
        k_ref[...] = jnp.concatenate([ckv, kpe], axis=1).astype(bf16)
        for s in range(ckv.shape[0] // tk):
            ct_ref[s] = ckv[s * tk:(s + 1) * tk].T.astype(bf16)


def _swap_halves(w):
    half = w.shape[-1] // 2
    return jnp.concatenate([w[..., half:], w[..., :half]], axis=-1)


def mla_shared_kv(x, g_in, w_dkv, g_kv, cos2, sin2, attn_tk=None):
    m, d = x.shape
    tm = _row_tile(cos2.shape[0], 512)
    nt = cos2.shape[0] // tm
    w_c = w_dkv[:, :KV_LORA].astype(bf16)
    w_pa = w_dkv[:, KV_LORA:].astype(bf16)
    w_pb = _swap_halves(w_pa)
    full = lambda r, c: pl.BlockSpec((r, c), lambda i: (0, 0))
    rows = lambda c: pl.BlockSpec((tm, c), lambda i: (i, 0))
    tab = pl.BlockSpec((tm, ROPE_DIM), lambda i: (i % nt, 0))
    out_specs = [rows(KV_LORA), rows(ROPE_DIM)]
    out_shape = [jax.ShapeDtypeStruct((m, KV_LORA), f32), jax.ShapeDtypeStruct((m, ROPE_DIM), f32)]
    if attn_tk is not None:
        assert tm % attn_tk == 0
        out_specs += [rows(KV_LORA + ROPE_DIM),
                      pl.BlockSpec((tm // attn_tk, KV_LORA, attn_tk), lambda i: (i, 0, 0))]
        out_shape += [jax.ShapeDtypeStruct((m, KV_LORA + ROPE_DIM), bf16),
                      jax.ShapeDtypeStruct((m // attn_tk, KV_LORA, attn_tk), bf16)]
    return pl.pallas_call(
        functools.partial(_mla_kv_kernel, tk=attn_tk),
        grid=(m // tm,),
        in_specs=[rows(d), full(1, d), full(d, KV_LORA), full(d, ROPE_DIM), full(d, ROPE_DIM),
                  full(1, KV_LORA), tab, tab],
        out_specs=out_specs,
        out_shape=out_shape,
        compiler_params=_cparams("parallel"),
        name="mla_shared_kv",
    )(x, g_in.reshape(1, d), w_c, w_pa, w_pb, g_kv.reshape(1, KV_LORA), cos2, sin2)


def _mla_q_kernel(x_ref, gpre_ref, wdq_ref, gq_ref, wn_ref, wpa_ref, wpb_ref, wuk_ref, cos_ref, sin_ref,
                  q_ref, *, heads, scale):
    h = _rms(x_ref[...], gpre_ref[...]).astype(bf16)
    qa = _rms(jnp.dot(h, wdq_ref[...], preferred_element_type=f32), gq_ref[...]).astype(bf16)
    nope = jnp.dot(qa, wn_ref[...], preferred_element_type=f32).astype(bf16)
    pa = jnp.dot(qa, wpa_ref[...], preferred_element_type=f32)
    pb = jnp.dot(qa, wpb_ref[...], preferred_element_type=f32)
    qpe = (pa * cos_ref[...] + pb * sin_ref[...]) * scale
    for hh in range(heads):
        qlat = jnp.dot(nope[:, hh * NOPE_DIM:(hh + 1) * NOPE_DIM], wuk_ref[hh], preferred_element_type=f32)
        q_ref[hh] = jnp.concatenate([qlat * scale, qpe[:, hh * ROPE_DIM:(hh + 1) * ROPE_DIM]],
                                    axis=1).astype(q_ref.dtype)


def mla_query(x, g_pre, w_dq, g_q, w_uq, w_uk, cos2h, sin2h, out_dtype):
    m, d = x.shape
    heads = MLA_HEADS
    q_lora = w_dq.shape[1]
    qk_head = NOPE_DIM + ROPE_DIM
    tm = _row_tile(cos2h.shape[0], 512)
    nt = cos2h.shape[0] // tm
    w3 = w_uq.reshape(q_lora, heads, qk_head)
    w_n = w3[:, :, :NOPE_DIM].reshape(q_lora, heads * NOPE_DIM).astype(bf16)
    w_pa3 = w3[:, :, NOPE_DIM:]
    w_pa = w_pa3.reshape(q_lora, heads * ROPE_DIM).astype(bf16)
    w_pb = _swap_halves(w_pa3).reshape(q_lora, heads * ROPE_DIM).astype(bf16)
    w_ukt = jnp.transpose(w_uk.reshape(KV_LORA, heads, NOPE_DIM), (1, 2, 0)).astype(bf16)
    scale = qk_head ** -0.5
    full = lambda *s: pl.BlockSpec(s, lambda i: (0,) * len(s))
    tab = pl.BlockSpec((tm, heads * ROPE_DIM), lambda i: (i % nt, 0))
    kern = functools.partial(_mla_q_kernel, heads=heads, scale=scale)
    return pl.pallas_call(
        kern,
        grid=(m // tm,),
        in_specs=[pl.BlockSpec((tm, d), lambda i: (i, 0)), full(1, d), full(d, q_lora), full(1, q_lora),
                  full(q_lora, heads * NOPE_DIM), full(q_lora, heads * ROPE_DIM), full(q_lora, heads * ROPE_DIM),
                  full(heads, NOPE_DIM, KV_LORA), tab, tab],
        out_specs=pl.BlockSpec((heads, tm, KV_LORA + ROPE_DIM), lambda i: (0, i, 0)),
        out_shape=jax.ShapeDtypeStruct((heads, m, KV_LORA + ROPE_DIM), out_dtype),
        compiler_params=_cparams("parallel"),
        name="mla_query",
    )(x, g_pre.reshape(1, d), w_dq.astype(bf16), g_q.reshape(1, q_lora), w_n, w_pa, w_pb, w_ukt, cos2h, sin2h)


def _softmax_update(s, pv_fn, m_scr, l_scr, acc_scr):
    m_prev = m_scr[...]
    m_new = jnp.maximum(m_prev, jnp.max(s, axis=-1, keepdims=True))
    alpha = jnp.exp(m_prev - m_new)
    p = jnp.exp(s - m_new[:, 0:1])
    l_scr[...] = alpha * l_scr[...] + jnp.sum(p, axis=-1, keepdims=True)
    acc_scr[...] = acc_scr[...] * alpha[:, 0:1] + pv_fn(p.astype(bf16))
    m_scr[...] = m_new


def _softmax_init(m_scr, l_scr, acc_scr):
    m_scr[...] = jnp.full_like(m_scr, NEG_BIG)
    l_scr[...] = jnp.zeros_like(l_scr)
    acc_scr[...] = jnp.zeros_like(acc_scr)


def _prompt_attn_kernel(q_ref, k_ref, kt_ref, wuvt_ref, o_ref, acc_scr, *, heads, qb):
    i = pl.program_id(1)
    tk = qb
    grp = heads
    k_idx = lax.broadcasted_iota(jnp.int32, (tk, qb), 0)
    q_idx = lax.broadcasted_iota(jnp.int32, (tk, qb), 1)
    causal = k_idx <= q_idx

    for h0 in range(0, heads, grp):
        qs = [q_ref[h0 + g] for g in range(grp)]

        def block(j, carry, masked):
            kj = k_ref[pl.ds(pl.multiple_of(j * tk, tk), tk), :]
            ktj = kt_ref[j]
            sts = [_mm_nt(kj, qs[g]) for g in range(grp)]
            if masked:
                sts = [jnp.where(causal, st, NEG_BIG) for st in sts]
            new = []
            for g in range(grp):
                m_prev, l_prev = carry[g]
                m_new = jnp.maximum(m_prev, jnp.max(sts[g], axis=0, keepdims=True))
                alpha = jnp.exp(m_prev - m_new)
                p = jnp.exp(sts[g] - m_new)
                l_new = alpha * l_prev + jnp.sum(p, axis=0, keepdims=True)
                acc_scr[g] = acc_scr[g] * alpha + jnp.dot(ktj, p.astype(bf16), preferred_element_type=f32)
                new.append((m_new, l_new))
            return tuple(new)

        acc_scr[...] = jnp.zeros_like(acc_scr)
        init = tuple((jnp.full((1, qb), NEG_BIG, f32), jnp.zeros((1, qb), f32)) for _ in range(grp))
        carry = lax.fori_loop(0, i, lambda j, c: block(j, c, False), init)
        carry = block(i, carry, True)
        for g in range(grp):
            o_t = (acc_scr[g] / carry[g][1]).astype(bf16)
            head_t = jnp.dot(wuvt_ref[h0 + g], o_t, preferred_element_type=f32)
            o_ref[:, (h0 + g) * V_HEAD_DIM:(h0 + g + 1) * V_HEAD_DIM] = head_t.T.astype(o_ref.dtype)


def prompt_attention(q, keys, kt, w_uvt, batch, seq):
    heads = q.shape[0]
    qk = q.shape[2]
    qb = kt.shape[2]
    nq = seq // qb
    kern = functools.partial(_prompt_attn_kernel, heads=heads, qb=qb)
    return pl.pallas_call(
        kern,
        grid=(batch, nq),
        in_specs=[
            pl.BlockSpec((heads, qb, qk), lambda b, i: (0, b * nq + i, 0)),
            pl.BlockSpec((seq, qk), lambda b, i: (b, 0)),
            pl.BlockSpec((nq, KV_LORA, qb), lambda b, i: (b, 0, 0)),
            pl.BlockSpec((heads, V_HEAD_DIM, KV_LORA), lambda b, i: (0, 0, 0)),
        ],
        out_specs=pl.BlockSpec((qb, heads * V_HEAD_DIM), lambda b, i: (b * nq + i, 0)),
        out_shape=jax.ShapeDtypeStruct((batch * seq, heads * V_HEAD_DIM), bf16),
        scratch_shapes=[pltpu.VMEM((heads, KV_LORA, qb), f32)],
        compiler_params=_cparams("parallel", "arbitrary"),
        name="prompt_attention",
    )(q, keys, kt, w_uvt)


DECODE_PAGES_PER_CHUNK = 16
DECODE_SLOTS = 4
DECODE_SUBCHAINS = 2


def _decode_attn_kernel(pt_ref, q_ref, cnew_ref, penew_ref, wuv_ref, ck_hbm, pe_hbm, o_ref,
                        kbuf, pbuf, ksem, psem, m_scr, l_scr, acc_scr, *, heads, seq, n_chunks, n_rows, pages):
    b = pl.program_id(0)
    r = heads * seq
    ahead = DECODE_SLOTS - 1
    sub = pages // DECODE_SUBCHAINS
    q_all = q_ref[...].reshape(r, KV_LORA + ROPE_DIM).astype(bf16)
    q = q_all[:, :KV_LORA]
    qp = q_all[:, KV_LORA:]

    def chunk_copies(row, ch, slot):
        out = []
        for p in range(pages):
            page = pt_ref[row, ch * pages + p]
            out.append(pltpu.make_async_copy(ck_hbm.at[page], kbuf.at[slot, p], ksem.at[slot]))
            out.append(pltpu.make_async_copy(pe_hbm.at[page], pbuf.at[slot, p], psem.at[slot]))
        return out

    def start_chunk(row, ch):
        @pl.when(row < n_rows)
        def _():
            slot = (row * n_chunks + ch) & (DECODE_SLOTS - 1)
            for cp in chunk_copies(row, ch, slot):
                cp.start()

    def start_ahead(ch):
        wrap = ch + ahead >= n_chunks
        start_chunk(jnp.where(wrap, b + 1, b), jnp.where(wrap, ch + ahead - n_chunks, ch + ahead))

    @pl.when(b == 0)
    def _():
        for t in range(min(ahead, n_chunks * n_rows)):
            start_chunk(jnp.int32(t // n_chunks), jnp.int32(t % n_chunks))

    for g in range(DECODE_SUBCHAINS):
        _softmax_init(m_scr.at[g], l_scr.at[g], acc_scr.at[g])

    def consume(ch, carry):
        slot = (b * n_chunks + ch) & (DECODE_SLOTS - 1)
        for cp in chunk_copies(b, ch, slot):
            cp.wait()
        start_ahead(ch)
        for g in range(DECODE_SUBCHAINS):
            kc = kbuf[slot, g * sub:(g + 1) * sub].reshape(sub * PAGE_SIZE, KV_LORA).astype(bf16)
            kp_t = jnp.concatenate([pbuf[slot, p] for p in range(g * sub, (g + 1) * sub)], axis=1)
            s = _mm_nt(q, kc) + _mm(qp, kp_t)
            _softmax_update(s, lambda pm, kc=kc: jnp.dot(pm, kc, preferred_element_type=f32),
                            m_scr.at[g], l_scr.at[g], acc_scr.at[g])
        return carry

    lax.fori_loop(0, n_chunks, consume, 0)

    kc = jnp.concatenate([cnew_ref[0], jnp.zeros((PAGE_SIZE - seq, KV_LORA), f32)], axis=0).astype(bf16)
    kp = jnp.concatenate([penew_ref[0], jnp.zeros((PAGE_SIZE - seq, ROPE_DIM), f32)], axis=0)
    s = _mm_nt(q, kc) + _mm_nt(qp, kp)
    q_t = lax.rem(lax.broadcasted_iota(jnp.int32, (r, PAGE_SIZE), 0), jnp.int32(seq))
    k_t = lax.broadcasted_iota(jnp.int32, (r, PAGE_SIZE), 1)
    s = jnp.where(k_t <= q_t, s, NEG_BIG)
    _softmax_update(s, lambda pm: jnp.dot(pm, kc, preferred_element_type=f32), m_scr.at[0], l_scr.at[0], acc_scr.at[0])
    m_all = m_scr[0]
    for g in range(1, DECODE_SUBCHAINS):
        m_all = jnp.maximum(m_all, m_scr[g])
    l_all = jnp.zeros_like(m_all)
    acc_all = jnp.zeros((r, KV_LORA), f32)
    for g in range(DECODE_SUBCHAINS):
        w = jnp.exp(m_scr[g] - m_all)
        l_all = l_all + w * l_scr[g]
        acc_all = acc_all + w[:, 0:1] * acc_scr[g]
    o_lat = (acc_all / l_all[:, 0:1]).astype(bf16)
    for hh in range(heads):
        full = jnp.dot(o_lat, wuv_ref[hh], preferred_element_type=f32)
        o_ref[:, hh * V_HEAD_DIM:(hh + 1) * V_HEAD_DIM] = full[hh * seq:(hh + 1) * seq]


def decode_attention(q, c_new, pe_new, cache_ckv, cache_kpe_t, page_table, w_uv3):
    heads = q.shape[0]
    db, seq, _ = c_new.shape
    n_pages = page_table.shape[1]
    pages = DECODE_PAGES_PER_CHUNK
    assert n_pages % pages == 0 and pages % DECODE_SUBCHAINS == 0
    n_chunks = n_pages // pages
    assert n_chunks >= DECODE_SLOTS - 1
    r = heads * seq
    kern = functools.partial(_decode_attn_kernel, heads=heads, seq=seq, n_chunks=n_chunks, n_rows=db, pages=pages)
    grid_spec = pltpu.PrefetchScalarGridSpec(
        num_scalar_prefetch=1,
        grid=(db,),
        in_specs=[
            pl.BlockSpec((heads, seq, KV_LORA + ROPE_DIM), lambda b, pt: (0, b, 0)),
            pl.BlockSpec((1, seq, KV_LORA), lambda b, pt: (b, 0, 0)),
            pl.BlockSpec((1, seq, ROPE_DIM), lambda b, pt: (b, 0, 0)),
            pl.BlockSpec((heads, KV_LORA, V_HEAD_DIM), lambda b, pt: (0, 0, 0)),
            pl.BlockSpec(memory_space=pl.ANY),
            pl.BlockSpec(memory_space=pl.ANY),
        ],
        out_specs=pl.BlockSpec((seq, heads * V_HEAD_DIM), lambda b, pt: (b, 0)),
        scratch_shapes=[
            pltpu.VMEM((DECODE_SLOTS, pages, PAGE_SIZE, KV_LORA), f32),
            pltpu.VMEM((DECODE_SLOTS, pages, ROPE_DIM, PAGE_SIZE), f32),
            pltpu.SemaphoreType.DMA((DECODE_SLOTS,)),
            pltpu.SemaphoreType.DMA((DECODE_SLOTS,)),
            pltpu.VMEM((DECODE_SUBCHAINS, r, LANES), f32),
            pltpu.VMEM((DECODE_SUBCHAINS, r, LANES), f32),
            pltpu.VMEM((DECODE_SUBCHAINS, r, KV_LORA), f32),
        ],
    )
    return pl.pallas_call(
        kern,
        grid_spec=grid_spec,
        out_shape=jax.ShapeDtypeStruct((db * seq, heads * V_HEAD_DIM), f32),
        compiler_params=_cparams("arbitrary"),
        name="decode_attention",
    )(page_table, q, c_new, pe_new, w_uv3, cache_ckv, cache_kpe_t)


def _rope_tables(first_pos, seq):
    half = ROPE_DIM // 2
    inv_freq = ROPE_THETA ** (-jnp.arange(half, dtype=f32) / half)
    pos = first_pos + jnp.arange(seq, dtype=jnp.int32)
    ang = pos.astype(f32)[:, None] * inv_freq[None, :]
    cos, sin = jnp.cos(ang), jnp.sin(ang)
    return jnp.concatenate([cos, cos], axis=1), jnp.concatenate([-sin, sin], axis=1)


def _run_group(x3, gdn_s0, conv0, past, wts):
    batch, seq, d = x3.shape
    n_a = wts["gdn_w_in"].shape[0]
    depth = wts["ffn_in_b"].shape[0]
    x = x3.reshape(batch * seq, d)
    past_len = 0 if past is None else past[2].shape[1] * PAGE_SIZE
    cos2, sin2 = _rope_tables(past_len, seq)
    if seq < LANES:
        cos2, sin2 = jnp.tile(cos2, (batch, 1)), jnp.tile(sin2, (batch, 1))
    cos2h, sin2h = jnp.tile(cos2, (1, MLA_HEADS)), jnp.tile(sin2, (1, MLA_HEADS))
    act_dtype = bf16 if past is None else f32
    states, convs = [], []
    c_new = pe_new = keys_bf = kt_bf = None
    attn_tk = _row_tile(seq, 256) if past is None else None
    for layer in range(depth):
        if layer == n_a:
            kv = mla_shared_kv(x, wts["mla_kvin_norm"], wts["mla_w_dkv"], wts["mla_kv_norm"], cos2, sin2, attn_tk)
            c_new, pe_new = kv[0], kv[1]
            if past is None:
                keys_bf, kt_bf = kv[2], kv[3]
        x = ffn_sublayer(x, wts["norm_pre"][layer, 0], wts["ffn_in_b"], wts["ffn_out_b"],
                         wts["norm_post"][layer, 0], layer, 0)
        if layer < n_a:
            proj = norm_proj(x, wts["norm_pre"][layer, 1], wts["gdn_w_in_b"][layer], act_dtype)
            proj3 = proj.reshape(batch, seq, -1)
            conv_dim = conv0.shape[-1]
            o, s_new = gdn_core(proj3, wts["gdn_conv_w"][layer], conv0[layer], gdn_s0[layer],
                                wts["gdn_dt_bias"][layer], wts["gdn_a_log"][layer], wts["gdn_norm"][layer], act_dtype)
            states.append(s_new)
            if seq >= CONV_W - 1:
                convs.append(proj3[:, seq - (CONV_W - 1):, :conv_dim].astype(conv0.dtype))
            else:
                prev = jnp.concatenate([conv0[layer].astype(f32), proj3[:, :, :conv_dim]], axis=1)
                convs.append(prev[:, -(CONV_W - 1):])
            x = out_proj_residual(o.reshape(batch * seq, -1), wts["gdn_w_out_b"][layer], x, wts["norm_post"][layer, 1])
        else:
            jb = layer - n_a
            q = mla_query(x, wts["norm_pre"][layer, 1], wts["mla_w_dq"][jb], wts["mla_q_norm"][jb],
                                    wts["mla_w_uq"][jb], wts["mla_w_uk"], cos2h, sin2h, act_dtype)
            if past is None:
                o = prompt_attention(q, keys_bf, kt_bf, wts["w_uvt"], batch, seq)
            else:
                o = decode_attention(q, c_new.reshape(batch, seq, -1), pe_new.reshape(batch, seq, -1),
                                     past[0], past[1], past[2], wts["w_uv3"])
            x = out_proj_residual(o, wts["mla_w_o_b"][jb], x, wts["norm_post"][layer, 1])
        x = ffn_sublayer(x, wts["norm_pre"][layer, 2], wts["ffn_in_b"], wts["ffn_out_b"],
                         wts["norm_post"][layer, 2], layer, 1)
    return (x.reshape(batch, seq, d), jnp.stack(states), jnp.stack(convs),
            c_new.reshape(batch, seq, -1), pe_new.reshape(batch, seq, -1))


def _pack_gdn_w_in(w_in, heads):
    main = 4 * heads * GDN_HEAD_DIM
    padw = ((0, 0), (0, 0), (0, LANES - heads))
    b = jnp.pad(w_in[..., main:main + heads], padw)
    a = jnp.pad(w_in[..., main + heads:main + 2 * heads], padw)
    return jnp.concatenate([w_in[..., :main], b, a], axis=-1).astype(bf16)


def kernel(x_prompt, x_sample, state_gdn, state_conv, cache_ckv, cache_kpe, page_table, norm_pre, norm_post, ffn_in, ffn_out, gdn_w_in, gdn_conv_w, gdn_dt_bias, gdn_a_log, gdn_norm, gdn_w_out, mla_kvin_norm, mla_w_dkv, mla_kv_norm, mla_w_uk, mla_w_uv, mla_w_dq, mla_q_norm, mla_w_uq, mla_w_o):
    n_a = gdn_w_in.shape[0]
    b = x_prompt.shape[0]
    heads = state_gdn.shape[2]
    wts = dict(
        norm_pre=norm_pre, norm_post=norm_post,
        ffn_in_b=ffn_in.astype(bf16), ffn_out_b=ffn_out.astype(bf16),
        gdn_w_in=gdn_w_in, gdn_w_in_b=_pack_gdn_w_in(gdn_w_in, heads), gdn_conv_w=gdn_conv_w,
        gdn_dt_bias=gdn_dt_bias, gdn_a_log=gdn_a_log, gdn_norm=gdn_norm, gdn_w_out_b=gdn_w_out.astype(bf16),
        mla_kvin_norm=mla_kvin_norm, mla_w_dkv=mla_w_dkv, mla_kv_norm=mla_kv_norm, mla_w_uk=mla_w_uk,
        w_uv3=jnp.transpose(mla_w_uv.reshape(KV_LORA, MLA_HEADS, V_HEAD_DIM), (1, 0, 2)).astype(bf16),
        w_uvt=jnp.transpose(mla_w_uv.reshape(KV_LORA, MLA_HEADS, V_HEAD_DIM), (1, 2, 0)).astype(bf16),
        mla_w_dq=mla_w_dq, mla_q_norm=mla_q_norm, mla_w_uq=mla_w_uq, mla_w_o_b=mla_w_o.astype(bf16),
    )
    zero_s = jnp.zeros((n_a, b) + state_gdn.shape[2:], state_gdn.dtype)
    zero_conv = jnp.zeros((n_a, b) + state_conv.shape[2:], state_conv.dtype)
    y_p, sg_p, cv_p, ckv_p, kpe_p = _run_group(x_prompt, zero_s, zero_conv, None, wts)
    y_s, sg_s, cv_s, ckv_s, kpe_s = _run_group(x_sample, state_gdn, state_conv,
                                               (cache_ckv, jnp.swapaxes(cache_kpe, 1, 2), page_table), wts)
    return (y_p, y_s, sg_p, cv_p, ckv_p, kpe_p, sg_s, cv_s, ckv_s, kpe_s)
```

```python
import functools
import math

import jax
import jax.numpy as jnp
from jax import lax
from jax.experimental import pallas as pl
from jax.experimental.pallas import tpu as pltpu

EPS = 1e-6
ROPE_THETA = 10000.0
GDN_HEADS = 8
GDN_HEAD_DIM = 128
CONV_W = 4
MLA_HEADS = 8
KV_LORA = 256
NOPE_DIM = 128
ROPE_DIM = 64
V_HEAD_DIM = 128
PAGE_SIZE = 128

LANES = 128
SUBLANES = 8
VMEM_LIMIT_BYTES = 56 * 1024 * 1024
GDN_CHUNK_ROWS = 128
FFN_ROW_TILE = 512
FFN_CHUNK = 512
NEG_BIG = -1e30

bf16 = jnp.bfloat16
f32 = jnp.float32


def _cparams(*sem):
    return pltpu.CompilerParams(dimension_semantics=sem, vmem_limit_bytes=VMEM_LIMIT_BYTES)


def _rms(x, g):
    return x * lax.rsqrt(jnp.mean(x * x, axis=-1, keepdims=True) + EPS) * g


def _sigmoid(x):
    return 1.0 / (1.0 + jnp.exp(-x))


def _mm(a, b):
    return jnp.dot(a.astype(bf16), b.astype(bf16), preferred_element_type=f32)


def _mm_nt(a, b):
    return lax.dot_general(a.astype(bf16), b.astype(bf16), (((1,), (1,)), ((), ())),
                           preferred_element_type=f32)


def _mm_tn(a, b):
    return lax.dot_general(a.astype(bf16), b.astype(bf16), (((0,), (0,)), ((), ())),
                           preferred_element_type=f32)


def _row_tile(m, cap):
    t = min(m, cap)
    while m % t:
        t //= 2
    return t


def _ffn_kernel(x_ref, gpre_ref, win_ref, wout_ref, gpost_ref, *rest, chunks):
    o_ref = rest[-1]
    f = wout_ref.shape[0]
    x = x_ref[...]
    if len(rest) > 1:
        a_ref, wmix_ref, gmix_ref = rest[:3]
        x = x + _rms(jnp.dot(a_ref[...].astype(bf16), wmix_ref[...], preferred_element_type=f32), gmix_ref[...])
    h = _rms(x, gpre_ref[...]).astype(bf16)
    acc = None
    for c0, c1 in chunks:
        g = jnp.dot(h, win_ref[:, c0:c1], preferred_element_type=f32)
        u = jnp.dot(h, win_ref[:, f + c0:f + c1], preferred_element_type=f32)
        a = (g * _sigmoid(g) * u).astype(bf16)
        part = jnp.dot(a, wout_ref[c0:c1, :], preferred_element_type=f32)
        acc = part if acc is None else acc + part
    o_ref[...] = x + 0.5 * _rms(acc, gpost_ref[...])


def _ffn_chunks(f, width):
    assert f % LANES == 0
    return tuple((c0, min(c0 + width, f)) for c0 in range(0, f, width))


def ffn_sublayer(x, g_pre, w_in, w_out, g_post, layer, k, mixer_out=None):
    m, d = x.shape
    f = w_out.shape[2]
    tm = _row_tile(m, FFN_ROW_TILE)
    resident = pl.Buffered(1)
    mix_specs, mix_args = [], []
    if mixer_out is not None:
        a, w_mix, g_mix = mixer_out
        kdim = a.shape[1]
        mix_specs = [pl.BlockSpec((tm, kdim), lambda i: (i, 0)),
                     pl.BlockSpec((kdim, d), lambda i: (0, 0), pipeline_mode=resident),
                     pl.BlockSpec((1, d), lambda i: (0, 0))]
        mix_args = [a, w_mix, g_mix.reshape(1, d)]
    return pl.pallas_call(
        functools.partial(_ffn_kernel, chunks=_ffn_chunks(f, FFN_CHUNK)),
        grid=(m // tm,),
        in_specs=[
            pl.BlockSpec((tm, d), lambda i: (i, 0)),
            pl.BlockSpec((1, d), lambda i: (0, 0)),
            pl.BlockSpec((None, None, d, 2 * f), lambda i: (layer, k, 0, 0), pipeline_mode=resident),
            pl.BlockSpec((None, None, f, d), lambda i: (layer, k, 0, 0), pipeline_mode=resident),
            pl.BlockSpec((1, d), lambda i: (0, 0)),
        ] + mix_specs,
        out_specs=pl.BlockSpec((tm, d), lambda i: (i, 0)),
        out_shape=jax.ShapeDtypeStruct((m, d), f32),
        compiler_params=_cparams("parallel"),
        name="ffn_sublayer",
    )(x, g_pre.reshape(1, d), w_in, w_out, g_post.reshape(1, d), *mix_args)


def _norm_proj_kernel(x_ref, g_ref, w_ref, o_ref, h_scr):
    @pl.when(pl.program_id(1) == 0)
    def _():
        h_scr[...] = _rms(x_ref[...], g_ref[...]).astype(bf16)

    o_ref[...] = jnp.dot(h_scr[...], w_ref[...], preferred_element_type=f32).astype(o_ref.dtype)


def norm_proj(x, g, w, out_dtype):
    m, d = x.shape
    n = w.shape[1]
    tm = _row_tile(m, 1024)
    tn = n // 2 if (n // 2) % LANES == 0 and n > 2304 else n
    return pl.pallas_call(
        _norm_proj_kernel,
        grid=(m // tm, n // tn),
        in_specs=[
            pl.BlockSpec((tm, d), lambda i, j: (i, 0)),
            pl.BlockSpec((1, d), lambda i, j: (0, 0)),
            pl.BlockSpec((d, tn), lambda i, j: (0, j)),
        ],
        out_specs=pl.BlockSpec((tm, tn), lambda i, j: (i, j)),
        out_shape=jax.ShapeDtypeStruct((m, n), out_dtype),
        scratch_shapes=[pltpu.VMEM((tm, d), bf16)],
        compiler_params=_cparams("parallel", "arbitrary"),
        name="gdn_in_proj",
    )(x, g.reshape(1, d), w)


def _gdn_kernel(proj_ref, convw_ref, conv0_ref, s0_ref, dtb_ref, alog_ref, og_ref,
                o_ref, s_out_ref, s_scr, xp_scr, *, heads, rows):
    n = pl.program_id(1)
    hd = GDN_HEAD_DIM
    c = GDN_CHUNK_ROWS
    qk_dim = heads * hd
    conv_dim = 3 * qk_dim
    z_off = conv_dim
    b_off = z_off + qk_dim
    a_off = b_off + LANES
    pad = SUBLANES

    @pl.when(n == 0)
    def _():
        s_scr[...] = s0_ref[0]
        xp_scr[0:pad, :] = conv0_ref[0]
        if rows < c:
            xp_scr[pad + rows:pad + c, :] = jnp.zeros((c - rows, conv_dim), f32)

    if rows == c:
        @pl.when(n > 0)
        def _():
            xp_scr[0:pad, :] = xp_scr[c:c + pad, :]

    xp_scr[pad:pad + rows, :] = proj_ref[0, :, 0:conv_dim].astype(f32)

    def gate_block(off):
        blk = proj_ref[0, :, off:off + LANES].astype(f32)
        if rows < c:
            blk = jnp.concatenate([blk, jnp.zeros((c - rows, LANES), f32)], axis=0)
        return blk

    row_i = lax.broadcasted_iota(jnp.int32, (c, c), 0)
    col_i = lax.broadcasted_iota(jnp.int32, (c, c), 1)
    tril = row_i >= col_i
    strict = row_i > col_i
    real = lax.broadcasted_iota(jnp.int32, (c, LANES), 0) < rows

    beta = jnp.where(real, _sigmoid(gate_block(b_off)), 0.0)
    a_in = gate_block(a_off) + dtb_ref[...]
    softplus = jnp.maximum(a_in, 0.0) + jnp.log(1.0 + jnp.exp(-jnp.abs(a_in)))
    g = jnp.where(real, -jnp.exp(alog_ref[...]) * softplus, 0.0)
    g_hi = g.astype(bf16)
    g_mid = (g - g_hi.astype(f32)).astype(bf16)
    g_lo = (g - g_hi.astype(f32) - g_mid.astype(f32)).astype(bf16)
    gsum = jnp.dot(jnp.where(tril, 1.0, 0.0).astype(bf16), jnp.concatenate([g_hi, g_mid, g_lo], axis=1),
                   preferred_element_type=f32)
    gcum = gsum[:, :LANES] + gsum[:, LANES:2 * LANES] + gsum[:, 2 * LANES:]
    gcum_t = gcum.T
    glast = gcum[c - 1:c, :]
    e_g = jnp.exp(gcum)
    e_kd = jnp.exp(glast - gcum)
    e_last = jnp.exp(glast)

    def conv_silu(col0):
        acc = xp_scr[pad - 3:pad - 3 + c, col0:col0 + hd] * convw_ref[0:1, col0:col0 + hd]
        for j in range(1, CONV_W):
            acc = acc + xp_scr[pad - 3 + j:pad - 3 + j + c, col0:col0 + hd] * convw_ref[j:j + 1, col0:col0 + hd]
        return acc * _sigmoid(acc)

    def l2norm(t):
        return t * lax.rsqrt(jnp.sum(t * t, axis=-1, keepdims=True) + EPS)

    def col(t, h):
        return jnp.broadcast_to(t[:, h:h + 1], (t.shape[0], hd))

    n_iter = max(int(math.ceil(math.log2(rows))) - 1, 0)

    hs = range(heads)
    q = [l2norm(conv_silu(h * hd)) * (hd ** -0.5) for h in hs]
    k = [l2norm(conv_silu(qk_dim + h * hd)) for h in hs]
    v = [conv_silu(2 * qk_dim + h * hd) for h in hs]
    beta_c = [col(beta, h) for h in hs]
    e_g_c = [col(e_g, h) for h in hs]
    decay = []
    for h in hs:
        diff = col(gcum, h) - jnp.broadcast_to(gcum_t[h:h + 1, :], (c, c))
        decay.append(jnp.where(tril, jnp.exp(jnp.where(tril, diff, 0.0)), 0.0))
    kb = [k[h] * beta_c[h] for h in hs]
    k_bf = [k[h].astype(bf16) for h in hs]
    pw = [-jnp.where(strict, _mm_nt(kb[h], k_bf[h]) * decay[h], 0.0) for h in hs]
    nm = list(pw)
    for _ in range(n_iter):
        pw_bf = [p.astype(bf16) for p in pw]
        pw = [_mm(pw_bf[h], pw_bf[h]) for h in hs]
        pw_bf = [p.astype(bf16) for p in pw]
        nm = [nm[h] + pw[h] + _mm(nm[h], pw_bf[h]) for h in hs]
    rhs = [jnp.concatenate([v[h] * beta_c[h], kb[h] * e_g_c[h]], axis=1) for h in hs]
    sol = [rhs[h] + _mm(nm[h], rhs[h]) for h in hs]
    qk = [_mm_nt(q[h], k_bf[h]) * decay[h] for h in hs]
    qg = [q[h] * e_g_c[h] for h in hs]
    kdec = [k[h] * col(e_kd, h) for h in hs]
    s_prev = [s_scr[h] for h in hs]
    s_bf = [s.astype(bf16) for s in s_prev]
    v_new = [sol[h][:, :hd] - _mm(sol[h][:, hd:], s_bf[h]) for h in hs]
    v_bf = [t.astype(bf16) for t in v_new]
    o = [_mm(qg[h], s_bf[h]) + _mm(qk[h], v_bf[h]) for h in hs]
    for h in hs:
        s_scr[h] = s_prev[h] * e_last[:, h:h + 1] + _mm_tn(kdec[h], v_bf[h])
    for h in hs:
        zh = proj_ref[0, :, z_off + h * hd:z_off + (h + 1) * hd].astype(f32)
        o_ref[0, :, h * hd:(h + 1) * hd] = (_rms(o[h][:rows], og_ref[...]) * (zh * _sigmoid(zh))).astype(o_ref.dtype)

    @pl.when(n == pl.num_programs(1) - 1)
    def _():
        s_out_ref[0] = s_scr[...]


def gdn_core(proj, conv_w, conv0, s0, dt_bias, a_log, onorm_g, out_dtype):
    b, l, p = proj.shape
    heads = s0.shape[1]
    hd = GDN_HEAD_DIM
    conv_dim = 3 * heads * hd
    rows = min(l, GDN_CHUNK_ROWS)
    assert l % rows == 0 and (rows == GDN_CHUNK_ROWS or l == rows)
    conv0p = jnp.concatenate([jnp.zeros((b, SUBLANES - (CONV_W - 1), conv_dim), f32), conv0.astype(f32)], axis=1)
    pad_l = lambda t: jnp.pad(t.astype(f32), (0, LANES - heads)).reshape(1, LANES)
    kern = functools.partial(_gdn_kernel, heads=heads, rows=rows)
    return pl.pallas_call(
        kern,
        grid=(b, l // rows),
        in_specs=[
            pl.BlockSpec((1, rows, p), lambda i, n: (i, n, 0)),
            pl.BlockSpec((CONV_W, conv_dim), lambda i, n: (0, 0)),
            pl.BlockSpec((1, SUBLANES, conv_dim), lambda i, n: (i, 0, 0)),
            pl.BlockSpec((1, heads, hd, hd), lambda i, n: (i, 0, 0, 0)),
            pl.BlockSpec((1, LANES), lambda i, n: (0, 0)),
            pl.BlockSpec((1, LANES), lambda i, n: (0, 0)),
            pl.BlockSpec((1, hd), lambda i, n: (0, 0)),
        ],
        out_specs=[
            pl.BlockSpec((1, rows, heads * hd), lambda i, n: (i, n, 0)),
            pl.BlockSpec((1, heads, hd, hd), lambda i, n: (i, 0, 0, 0)),
        ],
        out_shape=[
            jax.ShapeDtypeStruct((b, l, heads * hd), out_dtype),
            jax.ShapeDtypeStruct(s0.shape, s0.dtype),
        ],
        scratch_shapes=[
            pltpu.VMEM((heads, hd, hd), f32),
            pltpu.VMEM((GDN_CHUNK_ROWS + SUBLANES, conv_dim), f32),
        ],
        compiler_params=_cparams("parallel", "arbitrary"),
        name="gdn_core",
    )(proj, conv_w, conv0p, s0, pad_l(dt_bias), pad_l(a_log), onorm_g.reshape(1, hd))


def _mla_kv_kernel(x_ref, gin_ref, wc_ref, wpa_ref, wpb_ref, gkv_ref, cos_ref, sin_ref,
                   c_ref, pe_ref, *attn_refs, tk):
    h = _rms(x_ref[...], gin_ref[...]).astype(bf16)
    ckv = _rms(jnp.dot(h, wc_ref[...], preferred_element_type=f32), gkv_ref[...])
    pa = jnp.dot(h, wpa_ref[...], preferred_element_type=f32)
    pb = jnp.dot(h, wpb_ref[...], preferred_element_type=f32)
    kpe = pa * cos_ref[...] + pb * sin_ref[...]
    c_ref[...] = ckv
    pe_ref[...] = kpe
    if attn_refs:
        k_ref, ct_ref = attn_refs
        k_ref[...] = jnp.concatenate([ckv, kpe], axis=1).astype(bf16)
        for s in range(ckv.shape[0] // tk):
            ct_ref[s] = ckv[s * tk:(s + 1) * tk].T.astype(bf16)


def _swap_halves(w):
    half = w.shape[-1] // 2
    return jnp.concatenate([w[..., half:], w[..., :half]], axis=-1)


def mla_shared_kv(x, g_in, w_dkv, g_kv, cos2, sin2, attn_tk=None):
    m, d = x.shape
    tm = _row_tile(cos2.shape[0], 512)
    nt = cos2.shape[0] // tm
    w_c = w_dkv[:, :KV_LORA].astype(bf16)
    w_pa = w_dkv[:, KV_LORA:].astype(bf16)
    w_pb = _swap_halves(w_pa)
    full = lambda r, c: pl.BlockSpec((r, c), lambda i: (0, 0))
    rows = lambda c: pl.BlockSpec((tm, c), lambda i: (i, 0))
    tab = pl.BlockSpec((tm, ROPE_DIM), lambda i: (i % nt, 0))
    out_specs = [rows(KV_LORA), rows(ROPE_DIM)]
    out_shape = [jax.ShapeDtypeStruct((m, KV_LORA), f32), jax.ShapeDtypeStruct((m, ROPE_DIM), f32)]
    if attn_tk is not None:
        assert tm % attn_tk == 0
        out_specs += [rows(KV_LORA + ROPE_DIM),
                      pl.BlockSpec((tm // attn_tk, KV_LORA, attn_tk), lambda i: (i, 0, 0))]
        out_shape += [jax.ShapeDtypeStruct((m, KV_LORA + ROPE_DIM), bf16),
                      jax.ShapeDtypeStruct((m // attn_tk, KV_LORA, attn_tk), bf16)]
    return pl.pallas_call(
        functools.partial(_mla_kv_kernel, tk=attn_tk),
        grid=(m // tm,),
        in_specs=[rows(d), full(1, d), full(d, KV_LORA), full(d, ROPE_DIM), full(d, ROPE_DIM),
                  full(1, KV_LORA), tab, tab],
        out_specs=out_specs,
        out_shape=out_shape,
        compiler_params=_cparams("parallel"),
        name="mla_shared_kv",
    )(x, g_in.reshape(1, d), w_c, w_pa, w_pb, g_kv.reshape(1, KV_LORA), cos2, sin2)


def _mla_q_kernel(x_ref, gpre_ref, wdq_ref, gq_ref, wn_ref, wpa_ref, wpb_ref, wuk_ref, cos_ref, sin_ref,
                  q_ref, *, heads, scale):
    h = _rms(x_ref[...], gpre_ref[...]).astype(bf16)
    qa = _rms(jnp.dot(h, wdq_ref[...], preferred_element_type=f32), gq_ref[...]).astype(bf16)
    nope = jnp.dot(qa, wn_ref[...], preferred_element_type=f32).astype(bf16)
    pa = jnp.dot(qa, wpa_ref[...], preferred_element_type=f32)
    pb = jnp.dot(qa, wpb_ref[...], preferred_element_type=f32)
    qpe = (pa * cos_ref[...] + pb * sin_ref[...]) * scale
    for hh in range(heads):
        qlat = jnp.dot(nope[:, hh * NOPE_DIM:(hh + 1) * NOPE_DIM], wuk_ref[hh], preferred_element_type=f32)
        q_ref[hh] = jnp.concatenate([qlat * scale, qpe[:, hh * ROPE_DIM:(hh + 1) * ROPE_DIM]],
                                    axis=1).astype(q_ref.dtype)


def mla_query(x, g_pre, w_dq, g_q, w_uq, w_uk, cos2h, sin2h, out_dtype):
    m, d = x.shape
    heads = MLA_HEADS
    q_lora = w_dq.shape[1]
    qk_head = NOPE_DIM + ROPE_DIM
    tm = _row_tile(cos2h.shape[0], 512)
    nt = cos2h.shape[0] // tm
    w3 = w_uq.reshape(q_lora, heads, qk_head)
    w_n = w3[:, :, :NOPE_DIM].reshape(q_lora, heads * NOPE_DIM).astype(bf16)
    w_pa3 = w3[:, :, NOPE_DIM:]
    w_pa = w_pa3.reshape(q_lora, heads * ROPE_DIM).astype(bf16)
    w_pb = _swap_halves(w_pa3).reshape(q_lora, heads * ROPE_DIM).astype(bf16)
    w_ukt = jnp.transpose(w_uk.reshape(KV_LORA, heads, NOPE_DIM), (1, 2, 0)).astype(bf16)
    scale = qk_head ** -0.5
    full = lambda *s: pl.BlockSpec(s, lambda i: (0,) * len(s))
    tab = pl.BlockSpec((tm, heads * ROPE_DIM), lambda i: (i % nt, 0))
    kern = functools.partial(_mla_q_kernel, heads=heads, scale=scale)
    return pl.pallas_call(
        kern,
        grid=(m // tm,),
        in_specs=[pl.BlockSpec((tm, d), lambda i: (i, 0)), full(1, d), full(d, q_lora), full(1, q_lora),
                  full(q_lora, heads * NOPE_DIM), full(q_lora, heads * ROPE_DIM), full(q_lora, heads * ROPE_DIM),
                  full(heads, NOPE_DIM, KV_LORA), tab, tab],
        out_specs=pl.BlockSpec((heads, tm, KV_LORA + ROPE_DIM), lambda i: (0, i, 0)),
        out_shape=jax.ShapeDtypeStruct((heads, m, KV_LORA + ROPE_DIM), out_dtype),
        compiler_params=_cparams("parallel"),
        name="mla_query",
    )(x, g_pre.reshape(1, d), w_dq.astype(bf16), g_q.reshape(1, q_lora), w_n, w_pa, w_pb, w_ukt, cos2h, sin2h)


def _softmax_update(s, pv_fn, m_scr, l_scr, acc_scr):
    m_prev = m_scr[...]
    m_new = jnp.maximum(m_prev, jnp.max(s, axis=-1, keepdims=True))
    alpha = jnp.exp(m_prev - m_new)
    p = jnp.exp(s - m_new[:, 0:1])
    l_scr[...] = alpha * l_scr[...] + jnp.sum(p, axis=-1, keepdims=True)
    acc_scr[...] = acc_scr[...] * alpha[:, 0:1] + pv_fn(p.astype(bf16))
    m_scr[...] = m_new


def _softmax_init(m_scr, l_scr, acc_scr):
    m_scr[...] = jnp.full_like(m_scr, NEG_BIG)
    l_scr[...] = jnp.zeros_like(l_scr)
    acc_scr[...] = jnp.zeros_like(acc_scr)


def _prompt_attn_kernel(q_ref, k_ref, kt_ref, wuvt_ref, o_ref, acc_scr, *, heads, qb):
    i = pl.program_id(1)
    tk = qb
    grp = heads
    k_idx = lax.broadcasted_iota(jnp.int32, (tk, qb), 0)
    q_idx = lax.broadcasted_iota(jnp.int32, (tk, qb), 1)
    causal = k_idx <= q_idx

    for h0 in range(0, heads, grp):
        qs = [q_ref[h0 + g] for g in range(grp)]

        def block(j, carry, masked):
            kj = k_ref[pl.ds(pl.multiple_of(j * tk, tk), tk), :]
            ktj = kt_ref[j]
            sts = [_mm_nt(kj, qs[g]) for g in range(grp)]
            if masked:
                sts = [jnp.where(causal, st, NEG_BIG) for st in sts]
            new = []
            for g in range(grp):
                m_prev, l_prev = carry[g]
                m_new = jnp.maximum(m_prev, jnp.max(sts[g], axis=0, keepdims=True))
                alpha = jnp.exp(m_prev - m_new)
                p = jnp.exp(sts[g] - m_new)
                l_new = alpha * l_prev + jnp.sum(p, axis=0, keepdims=True)
                acc_scr[g] = acc_scr[g] * alpha + jnp.dot(ktj, p.astype(bf16), preferred_element_type=f32)
                new.append((m_new, l_new))
            return tuple(new)

        acc_scr[...] = jnp.zeros_like(acc_scr)
        init = tuple((jnp.full((1, qb), NEG_BIG, f32), jnp.zeros((1, qb), f32)) for _ in range(grp))
        carry = lax.fori_loop(0, i, lambda j, c: block(j, c, False), init)
        carry = block(i, carry, True)
        for g in range(grp):
            o_t = (acc_scr[g] / carry[g][1]).astype(bf16)
            head_t = jnp.dot(wuvt_ref[h0 + g], o_t, preferred_element_type=f32)
            o_ref[:, (h0 + g) * V_HEAD_DIM:(h0 + g + 1) * V_HEAD_DIM] = head_t.T.astype(o_ref.dtype)


def prompt_attention(q, keys, kt, w_uvt, batch, seq):
    heads = q.shape[0]
    qk = q.shape[2]
    qb = kt.shape[2]
    nq = seq // qb
    kern = functools.partial(_prompt_attn_kernel, heads=heads, qb=qb)
    return pl.pallas_call(
        kern,
        grid=(batch, nq),
        in_specs=[
            pl.BlockSpec((heads, qb, qk), lambda b, i: (0, b * nq + i, 0)),
            pl.BlockSpec((seq, qk), lambda b, i: (b, 0)),
            pl.BlockSpec((nq, KV_LORA, qb), lambda b, i: (b, 0, 0)),
            pl.BlockSpec((heads, V_HEAD_DIM, KV_LORA), lambda b, i: (0, 0, 0)),
        ],
        out_specs=pl.BlockSpec((qb, heads * V_HEAD_DIM), lambda b, i: (b * nq + i, 0)),
        out_shape=jax.ShapeDtypeStruct((batch * seq, heads * V_HEAD_DIM), bf16),
        scratch_shapes=[pltpu.VMEM((heads, KV_LORA, qb), f32)],
        compiler_params=_cparams("parallel", "arbitrary"),
        name="prompt_attention",
    )(q, keys, kt, w_uvt)


DECODE_PAGES_PER_CHUNK = 16
DECODE_SLOTS = 4
DECODE_SUBCHAINS = 2


def _decode_attn_kernel(pt_ref, q_ref, cnew_ref, penew_ref, wuv_ref, ck_hbm, pe_hbm, o_ref,
                        kbuf, pbuf, ksem, psem, m_scr, l_scr, acc_scr, *, heads, seq, n_chunks, n_rows, pages):
    b = pl.program_id(0)
    r = heads * seq
    ahead = DECODE_SLOTS - 1
    sub = pages // DECODE_SUBCHAINS
    q_all = q_ref[...].reshape(r, KV_LORA + ROPE_DIM).astype(bf16)
    q = q_all[:, :KV_LORA]
    qp = q_all[:, KV_LORA:]

    def chunk_copies(row, ch, slot):
        out = []
        for p in range(pages):
            page = pt_ref[row, ch * pages + p]
            out.append(pltpu.make_async_copy(ck_hbm.at[page], kbuf.at[slot, p], ksem.at[slot]))
            out.append(pltpu.make_async_copy(pe_hbm.at[page], pbuf.at[slot, p], psem.at[slot]))
        return out

    def start_chunk(row, ch):
        @pl.when(row < n_rows)
        def _():
            slot = (row * n_chunks + ch) & (DECODE_SLOTS - 1)
            for cp in chunk_copies(row, ch, slot):
                cp.start()

    def start_ahead(ch):
        wrap = ch + ahead >= n_chunks
        start_chunk(jnp.where(wrap, b + 1, b), jnp.where(wrap, ch + ahead - n_chunks, ch + ahead))

    @pl.when(b == 0)
    def _():
        for t in range(min(ahead, n_chunks * n_rows)):
            start_chunk(jnp.int32(t // n_chunks), jnp.int32(t % n_chunks))

    for g in range(DECODE_SUBCHAINS):
        _softmax_init(m_scr.at[g], l_scr.at[g], acc_scr.at[g])

    def consume(ch, carry):
        slot = (b * n_chunks + ch) & (DECODE_SLOTS - 1)
        for cp in chunk_copies(b, ch, slot):
            cp.wait()
        start_ahead(ch)
        for g in range(DECODE_SUBCHAINS):
            kc = kbuf[slot, g * sub:(g + 1) * sub].reshape(sub * PAGE_SIZE, KV_LORA).astype(bf16)
            kp_t = jnp.concatenate([pbuf[slot, p] for p in range(g * sub, (g + 1) * sub)], axis=1)
            s = _mm_nt(q, kc) + _mm(qp, kp_t)
            _softmax_update(s, lambda pm, kc=kc: jnp.dot(pm, kc, preferred_element_type=f32),
                            m_scr.at[g], l_scr.at[g], acc_scr.at[g])
        return carry

    lax.fori_loop(0, n_chunks, consume, 0)

    kc = jnp.concatenate([cnew_ref[0], jnp.zeros((PAGE_SIZE - seq, KV_LORA), f32)], axis=0).astype(bf16)
    kp = jnp.concatenate([penew_ref[0], jnp.zeros((PAGE_SIZE - seq, ROPE_DIM), f32)], axis=0)
    s = _mm_nt(q, kc) + _mm_nt(qp, kp)
    q_t = lax.rem(lax.broadcasted_iota(jnp.int32, (r, PAGE_SIZE), 0), jnp.int32(seq))
    k_t = lax.broadcasted_iota(jnp.int32, (r, PAGE_SIZE), 1)
    s = jnp.where(k_t <= q_t, s, NEG_BIG)
    _softmax_update(s, lambda pm: jnp.dot(pm, kc, preferred_element_type=f32), m_scr.at[0], l_scr.at[0], acc_scr.at[0])
    m_all = m_scr[0]
    for g in range(1, DECODE_SUBCHAINS):
        m_all = jnp.maximum(m_all, m_scr[g])
    l_all = jnp.zeros_like(m_all)
    acc_all = jnp.zeros((r, KV_LORA), f32)
    for g in range(DECODE_SUBCHAINS):
        w = jnp.exp(m_scr[g] - m_all)
        l_all = l_all + w * l_scr[g]
        acc_all = acc_all + w[:, 0:1] * acc_scr[g]
    o_lat = (acc_all / l_all[:, 0:1]).astype(bf16)
    for hh in range(heads):
        full = jnp.dot(o_lat, wuv_ref[hh], preferred_element_type=f32)
        o_ref[:, hh * V_HEAD_DIM:(hh + 1) * V_HEAD_DIM] = full[hh * seq:(hh + 1) * seq]


def decode_attention(q, c_new, pe_new, cache_ckv, cache_kpe_t, page_table, w_uv3):
    heads = q.shape[0]
    db, seq, _ = c_new.shape
    n_pages = page_table.shape[1]
    pages = DECODE_PAGES_PER_CHUNK
    assert n_pages % pages == 0 and pages % DECODE_SUBCHAINS == 0
    n_chunks = n_pages // pages
    assert n_chunks >= DECODE_SLOTS - 1
    r = heads * seq
    kern = functools.partial(_decode_attn_kernel, heads=heads, seq=seq, n_chunks=n_chunks, n_rows=db, pages=pages)
    grid_spec = pltpu.PrefetchScalarGridSpec(
        num_scalar_prefetch=1,
        grid=(db,),
        in_specs=[
            pl.BlockSpec((heads, seq, KV_LORA + ROPE_DIM), lambda b, pt: (0, b, 0)),
            pl.BlockSpec((1, seq, KV_LORA), lambda b, pt: (b, 0, 0)),
            pl.BlockSpec((1, seq, ROPE_DIM), lambda b, pt: (b, 0, 0)),
            pl.BlockSpec((heads, KV_LORA, V_HEAD_DIM), lambda b, pt: (0, 0, 0)),
            pl.BlockSpec(memory_space=pl.ANY),
            pl.BlockSpec(memory_space=pl.ANY),
        ],
        out_specs=pl.BlockSpec((seq, heads * V_HEAD_DIM), lambda b, pt: (b, 0)),
        scratch_shapes=[
            pltpu.VMEM((DECODE_SLOTS, pages, PAGE_SIZE, KV_LORA), f32),
            pltpu.VMEM((DECODE_SLOTS, pages, ROPE_DIM, PAGE_SIZE), f32),
            pltpu.SemaphoreType.DMA((DECODE_SLOTS,)),
            pltpu.SemaphoreType.DMA((DECODE_SLOTS,)),
            pltpu.VMEM((DECODE_SUBCHAINS, r, LANES), f32),
            pltpu.VMEM((DECODE_SUBCHAINS, r, LANES), f32),
            pltpu.VMEM((DECODE_SUBCHAINS, r, KV_LORA), f32),
        ],
    )
    return pl.pallas_call(
        kern,
        grid_spec=grid_spec,
        out_shape=jax.ShapeDtypeStruct((db * seq, heads * V_HEAD_DIM), f32),
        compiler_params=_cparams("arbitrary"),
        name="decode_attention",
    )(page_table, q, c_new, pe_new, w_uv3, cache_ckv, cache_kpe_t)


def _rope_tables(first_pos, seq):
    half = ROPE_DIM // 2
    inv_freq = ROPE_THETA ** (-jnp.arange(half, dtype=f32) / half)
    pos = first_pos + jnp.arange(seq, dtype=jnp.int32)
    ang = pos.astype(f32)[:, None] * inv_freq[None, :]
    cos, sin = jnp.cos(ang), jnp.sin(ang)
    return jnp.concatenate([cos, cos], axis=1), jnp.concatenate([-sin, sin], axis=1)


def _run_group(x3, gdn_s0, conv0, past, wts):
    batch, seq, d = x3.shape
    n_a = wts["gdn_w_in"].shape[0]
    depth = wts["ffn_in_b"].shape[0]
    x = x3.reshape(batch * seq, d)
    past_len = 0 if past is None else past[2].shape[1] * PAGE_SIZE
    cos2, sin2 = _rope_tables(past_len, seq)
    if seq < LANES:
        cos2, sin2 = jnp.tile(cos2, (batch, 1)), jnp.tile(sin2, (batch, 1))
    cos2h, sin2h = jnp.tile(cos2, (1, MLA_HEADS)), jnp.tile(sin2, (1, MLA_HEADS))
    act_dtype = bf16 if past is None else f32
    states, convs = [], []
    c_new = pe_new = keys_bf = kt_bf = None
    attn_tk = _row_tile(seq, 256) if past is None else None
    for layer in range(depth):
        if layer == n_a:
            kv = mla_shared_kv(x, wts["mla_kvin_norm"], wts["mla_w_dkv"], wts["mla_kv_norm"], cos2, sin2, attn_tk)
            c_new, pe_new = kv[0], kv[1]
            if past is None:
                keys_bf, kt_bf = kv[2], kv[3]
        x = ffn_sublayer(x, wts["norm_pre"][layer, 0], wts["ffn_in_b"], wts["ffn_out_b"],
                         wts["norm_post"][layer, 0], layer, 0)
        if layer < n_a:
            proj = norm_proj(x, wts["norm_pre"][layer, 1], wts["gdn_w_in_b"][layer], act_dtype)
            proj3 = proj.reshape(batch, seq, -1)
            conv_dim = conv0.shape[-1]
            o, s_new = gdn_core(proj3, wts["gdn_conv_w"][layer], conv0[layer], gdn_s0[layer],
                                wts["gdn_dt_bias"][layer], wts["gdn_a_log"][layer], wts["gdn_norm"][layer], act_dtype)
            states.append(s_new)
            if seq >= CONV_W - 1:
                convs.append(proj3[:, seq - (CONV_W - 1):, :conv_dim].astype(conv0.dtype))
            else:
                prev = jnp.concatenate([conv0[layer].astype(f32), proj3[:, :, :conv_dim]], axis=1)
                convs.append(prev[:, -(CONV_W - 1):])
            mixer_out = (o.reshape(batch * seq, -1), wts["gdn_w_out_b"][layer], wts["norm_post"][layer, 1])
        else:
            jb = layer - n_a
            q = mla_query(x, wts["norm_pre"][layer, 1], wts["mla_w_dq"][jb], wts["mla_q_norm"][jb],
                                    wts["mla_w_uq"][jb], wts["mla_w_uk"], cos2h, sin2h, act_dtype)
            if past is None:
                o = prompt_attention(q, keys_bf, kt_bf, wts["w_uvt"], batch, seq)
            else:
                o = decode_attention(q, c_new.reshape(batch, seq, -1), pe_new.reshape(batch, seq, -1),
                                     past[0], past[1], past[2], wts["w_uv3"])
            mixer_out = (o, wts["mla_w_o_b"][jb], wts["norm_post"][layer, 1])
        x = ffn_sublayer(x, wts["norm_pre"][layer, 2], wts["ffn_in_b"], wts["ffn_out_b"],
                         wts["norm_post"][layer, 2], layer, 1, mixer_out)
    return (x.reshape(batch, seq, d), jnp.stack(states), jnp.stack(convs),
            c_new.reshape(batch, seq, -1), pe_new.reshape(batch, seq, -1))


def _pack_gdn_w_in(w_in, heads):
    main = 4 * heads * GDN_HEAD_DIM
    padw = ((0, 0), (0, 0), (0, LANES - heads))
    b = jnp.pad(w_in[..., main:main + heads], padw)
    a = jnp.pad(w_in[..., main + heads:main + 2 * heads], padw)
    return jnp.concatenate([w_in[..., :main], b, a], axis=-1).astype(bf16)


def kernel(x_prompt, x_sample, state_gdn, state_conv, cache_ckv, cache_kpe, page_table, norm_pre, norm_post, ffn_in, ffn_out, gdn_w_in, gdn_conv_w, gdn_dt_bias, gdn_a_log, gdn_norm, gdn_w_out, mla_kvin_norm, mla_w_dkv, mla_kv_norm, mla_w_uk, mla_w_uv, mla_w_dq, mla_q_norm, mla_w_uq, mla_w_o):
    n_a = gdn_w_in.shape[0]
    b = x_prompt.shape[0]
    heads = state_gdn.shape[2]
    wts = dict(
        norm_pre=norm_pre, norm_post=norm_post,
        ffn_in_b=ffn_in.astype(bf16), ffn_out_b=ffn_out.astype(bf16),
        gdn_w_in=gdn_w_in, gdn_w_in_b=_pack_gdn_w_in(gdn_w_in, heads), gdn_conv_w=gdn_conv_w,
        gdn_dt_bias=gdn_dt_bias, gdn_a_log=gdn_a_log, gdn_norm=gdn_norm, gdn_w_out_b=gdn_w_out.astype(bf16),
        mla_kvin_norm=mla_kvin_norm, mla_w_dkv=mla_w_dkv, mla_kv_norm=mla_kv_norm, mla_w_uk=mla_w_uk,
        w_uv3=jnp.transpose(mla_w_uv.reshape(KV_LORA, MLA_HEADS, V_HEAD_DIM), (1, 0, 2)).astype(bf16),
        w_uvt=jnp.transpose(mla_w_uv.reshape(KV_LORA, MLA_HEADS, V_HEAD_DIM), (1, 2, 0)).astype(bf16),
        mla_w_dq=mla_w_dq, mla_q_norm=mla_q_norm, mla_w_uq=mla_w_uq, mla_w_o_b=mla_w_o.astype(bf16),
    )
    zero_s = jnp.zeros((n_a, b) + state_gdn.shape[2:], state_gdn.dtype)
    zero_conv = jnp.zeros((n_a, b) + state_conv.shape[2:], state_conv.dtype)
    y_p, sg_p, cv_p, ckv_p, kpe_p = _run_group(x_prompt, zero_s, zero_conv, None, wts)
    y_s, sg_s, cv_s, ckv_s, kpe_s = _run_group(x_sample, state_gdn, state_conv,
                                               (cache_ckv, jnp.swapaxes(cache_kpe, 1, 2), page_table), wts)
    return (y_p, y_s, sg_p, cv_p, ckv_p, kpe_p, sg_s, cv_s, ckv_s, kpe_s)
```

```python
import functools
import math

import jax
import jax.numpy as jnp
from jax import lax
from jax.experimental import pallas as pl
from jax.experimental.pallas import tpu as pltpu

EPS = 1e-6
ROPE_THETA = 10000.0
GDN_HEADS = 8
GDN_HEAD_DIM = 128
CONV_W = 4
MLA_HEADS = 8
KV_LORA = 256
NOPE_DIM = 128
ROPE_DIM = 64
V_HEAD_DIM = 128
PAGE_SIZE = 128

LANES = 128
SUBLANES = 8
VMEM_LIMIT_BYTES = 56 * 1024 * 1024
GDN_CHUNK_ROWS = 128
GDN_SHORT_CHUNK_ROWS = 16
FFN_ROW_TILE = 1024
FFN_ROW_SPLIT = 2
FFN_CHUNK = 512
NEG_BIG = -1e30

bf16 = jnp.bfloat16
f32 = jnp.float32


def _cparams(*sem):
    return pltpu.CompilerParams(dimension_semantics=sem, vmem_limit_bytes=VMEM_LIMIT_BYTES)


def _rms(x, g):
    return x * lax.rsqrt(jnp.mean(x * x, axis=-1, keepdims=True) + EPS) * g


def _sigmoid(x):
    return 1.0 / (1.0 + jnp.exp(-x))


def _mm(a, b):
    return jnp.dot(a.astype(bf16), b.astype(bf16), preferred_element_type=f32)


def _mm_nt(a, b):
    return lax.dot_general(a.astype(bf16), b.astype(bf16), (((1,), (1,)), ((), ())),
                           preferred_element_type=f32)


def _mm_tn(a, b):
    return lax.dot_general(a.astype(bf16), b.astype(bf16), (((0,), (0,)), ((), ())),
                           preferred_element_type=f32)


def _row_tile(m, cap):
    t = min(m, cap)
    while m % t:
        t //= 2
    return t


def _ffn_kernel(x_ref, gpre_ref, win_ref, wout_ref, gpost_ref, *rest, chunks):
    o_ref = rest[-1]
    f = wout_ref.shape[0]
    tm = x_ref.shape[0]
    n_split = FFN_ROW_SPLIT if tm % (FFN_ROW_SPLIT * SUBLANES * 2) == 0 else 1
    rows = [(i * tm // n_split, (i + 1) * tm // n_split) for i in range(n_split)]
    xs = [x_ref[r0:r1, :] for r0, r1 in rows]
    if len(rest) > 1:
        a_ref, wmix_ref, gmix_ref = rest[:3]
        xs = [x + _rms(jnp.dot(a_ref[r0:r1, :].astype(bf16), wmix_ref[...], preferred_element_type=f32), gmix_ref[...])
              for x, (r0, r1) in zip(xs, rows)]
    hs = [_rms(x, gpre_ref[...]).astype(bf16) for x in xs]
    accs = [None] * n_split
    for c0, c1 in chunks:
        for i in range(n_split):
            g = jnp.dot(hs[i], win_ref[:, c0:c1], preferred_element_type=f32)
            u = jnp.dot(hs[i], win_ref[:, f + c0:f + c1], preferred_element_type=f32)
            a = (g * _sigmoid(g) * u).astype(bf16)
            part = jnp.dot(a, wout_ref[c0:c1, :], preferred_element_type=f32)
            accs[i] = part if accs[i] is None else accs[i] + part
    for x, acc, (r0, r1) in zip(xs, accs, rows):
        o_ref[r0:r1, :] = x + 0.5 * _rms(acc, gpost_ref[...])


def _ffn_chunks(f, width):
    assert f % LANES == 0
    return tuple((c0, min(c0 + width, f)) for c0 in range(0, f, width))


def ffn_sublayer(x, g_pre, w_in, w_out, g_post, layer, k, mixer_out=None):
    m, d = x.shape
    f = w_out.shape[2]
    tm = _row_tile(m, FFN_ROW_TILE)
    resident = pl.Buffered(1)
    mix_specs, mix_args = [], []
    if mixer_out is not None:
        a, w_mix, g_mix = mixer_out
        kdim = a.shape[1]
        mix_specs = [pl.BlockSpec((tm, kdim), lambda i: (i, 0)),
                     pl.BlockSpec((kdim, d), lambda i: (0, 0), pipeline_mode=resident),
                     pl.BlockSpec((1, d), lambda i: (0, 0))]
        mix_args = [a, w_mix, g_mix.reshape(1, d)]
    return pl.pallas_call(
        functools.partial(_ffn_kernel, chunks=_ffn_chunks(f, FFN_CHUNK)),
        grid=(m // tm,),
        in_specs=[
            pl.BlockSpec((tm, d), lambda i: (i, 0)),
            pl.BlockSpec((1, d), lambda i: (0, 0)),
            pl.BlockSpec((None, None, d, 2 * f), lambda i: (layer, k, 0, 0), pipeline_mode=resident),
            pl.BlockSpec((None, None, f, d), lambda i: (layer, k, 0, 0), pipeline_mode=resident),
            pl.BlockSpec((1, d), lambda i: (0, 0)),
        ] + mix_specs,
        out_specs=pl.BlockSpec((tm, d), lambda i: (i, 0)),
        out_shape=jax.ShapeDtypeStruct((m, d), f32),
        compiler_params=_cparams("parallel"),
        name="ffn_sublayer",
    )(x, g_pre.reshape(1, d), w_in, w_out, g_post.reshape(1, d), *mix_args)


def _norm_proj_kernel(x_ref, g_ref, w_ref, o_ref, h_scr):
    @pl.when(pl.program_id(1) == 0)
    def _():
        h_scr[...] = _rms(x_ref[...], g_ref[...]).astype(bf16)

    o_ref[...] = jnp.dot(h_scr[...], w_ref[...], preferred_element_type=f32).astype(o_ref.dtype)


def norm_proj(x, g, w, out_dtype):
    m, d = x.shape
    n = w.shape[1]
    tm = _row_tile(m, 1024)
    tn = n // 2 if (n // 2) % LANES == 0 and n > 2304 else n
    return pl.pallas_call(
        _norm_proj_kernel,
        grid=(m // tm, n // tn),
        in_specs=[
            pl.BlockSpec((tm, d), lambda i, j: (i, 0)),
            pl.BlockSpec((1, d), lambda i, j: (0, 0)),
            pl.BlockSpec((d, tn), lambda i, j: (0, j)),
        ],
        out_specs=pl.BlockSpec((tm, tn), lambda i, j: (i, j)),
        out_shape=jax.ShapeDtypeStruct((m, n), out_dtype),
        scratch_shapes=[pltpu.VMEM((tm, d), bf16)],
        compiler_params=_cparams("parallel", "arbitrary"),
        name="gdn_in_proj",
    )(x, g.reshape(1, d), w)


def _gdn_kernel(proj_ref, convw_ref, conv0_ref, s0_ref, dtb_ref, alog_ref, og_ref,
                o_ref, s_out_ref, s_scr, xp_scr, *, heads, rows, chunk):
    n = pl.program_id(1)
    hd = GDN_HEAD_DIM
    c = chunk
    qk_dim = heads * hd
    conv_dim = 3 * qk_dim
    z_off = conv_dim
    b_off = z_off + qk_dim
    a_off = b_off + LANES
    pad = SUBLANES

    @pl.when(n == 0)
    def _():
        s_scr[...] = s0_ref[0]
        xp_scr[0:pad, :] = conv0_ref[0]
        if rows < c:
            xp_scr[pad + rows:pad + c, :] = jnp.zeros((c - rows, conv_dim), f32)

    if rows == c:
        @pl.when(n > 0)
        def _():
            xp_scr[0:pad, :] = xp_scr[c:c + pad, :]

    xp_scr[pad:pad + rows, :] = proj_ref[0, :, 0:conv_dim].astype(f32)

    def gate_block(off):
        blk = proj_ref[0, :, off:off + LANES].astype(f32)
        if rows < c:
            blk = jnp.concatenate([blk, jnp.zeros((c - rows, LANES), f32)], axis=0)
        return blk

    row_i = lax.broadcasted_iota(jnp.int32, (c, c), 0)
    col_i = lax.broadcasted_iota(jnp.int32, (c, c), 1)
    tril = row_i >= col_i
    strict = row_i > col_i
    real = lax.broadcasted_iota(jnp.int32, (c, LANES), 0) < rows

    beta = jnp.where(real, _sigmoid(gate_block(b_off)), 0.0)
    a_in = gate_block(a_off) + dtb_ref[...]
    softplus = jnp.maximum(a_in, 0.0) + jnp.log(1.0 + jnp.exp(-jnp.abs(a_in)))
    g = jnp.where(real, -jnp.exp(alog_ref[...]) * softplus, 0.0)
    g_hi = g.astype(bf16)
    g_mid = (g - g_hi.astype(f32)).astype(bf16)
    g_lo = (g - g_hi.astype(f32) - g_mid.astype(f32)).astype(bf16)
    gsum = jnp.dot(jnp.where(tril, 1.0, 0.0).astype(bf16), jnp.concatenate([g_hi, g_mid, g_lo], axis=1),
                   preferred_element_type=f32)
    gcum = gsum[:, :LANES] + gsum[:, LANES:2 * LANES] + gsum[:, 2 * LANES:]
    gcum_t = gcum.T
    glast = gcum[c - 1:c, :]
    e_g = jnp.exp(gcum)
    e_kd = jnp.exp(glast - gcum)
    e_last = jnp.exp(glast)

    def conv_silu(col0):
        acc = xp_scr[pad - 3:pad - 3 + c, col0:col0 + hd] * convw_ref[0:1, col0:col0 + hd]
        for j in range(1, CONV_W):
            acc = acc + xp_scr[pad - 3 + j:pad - 3 + j + c, col0:col0 + hd] * convw_ref[j:j + 1, col0:col0 + hd]
        return acc * _sigmoid(acc)

    def l2norm(t):
        return t * lax.rsqrt(jnp.sum(t * t, axis=-1, keepdims=True) + EPS)

    def col(t, h):
        return jnp.broadcast_to(t[:, h:h + 1], (t.shape[0], hd))

    n_iter = max(int(math.ceil(math.log2(rows))) - 1, 0)

    hs = range(heads)
    q = [l2norm(conv_silu(h * hd)) * (hd ** -0.5) for h in hs]
    k = [l2norm(conv_silu(qk_dim + h * hd)) for h in hs]
    v = [conv_silu(2 * qk_dim + h * hd) for h in hs]
    beta_c = [col(beta, h) for h in hs]
    e_g_c = [col(e_g, h) for h in hs]
    decay = []
    for h in hs:
        diff = jnp.broadcast_to(gcum[:, h:h + 1], (c, c)) - jnp.broadcast_to(gcum_t[h:h + 1, :], (c, c))
        decay.append(jnp.where(tril, jnp.exp(jnp.where(tril, diff, 0.0)), 0.0))
    kb = [k[h] * beta_c[h] for h in hs]
    k_bf = [k[h].astype(bf16) for h in hs]
    pw = [-jnp.where(strict, _mm_nt(kb[h], k_bf[h]) * decay[h], 0.0) for h in hs]
    nm = list(pw)
    for _ in range(n_iter):
        pw_bf = [p.astype(bf16) for p in pw]
        pw = [_mm(pw_bf[h], pw_bf[h]) for h in hs]
        pw_bf = [p.astype(bf16) for p in pw]
        nm = [nm[h] + pw[h] + _mm(nm[h], pw_bf[h]) for h in hs]
    rhs = [jnp.concatenate([v[h] * beta_c[h], kb[h] * e_g_c[h]], axis=1) for h in hs]
    sol = [rhs[h] + _mm(nm[h], rhs[h]) for h in hs]
    qk = [_mm_nt(q[h], k_bf[h]) * decay[h] for h in hs]
    qg = [q[h] * e_g_c[h] for h in hs]
    kdec = [k[h] * col(e_kd, h) for h in hs]
    s_prev = [s_scr[h] for h in hs]
    s_bf = [s.astype(bf16) for s in s_prev]
    v_new = [sol[h][:, :hd] - _mm(sol[h][:, hd:], s_bf[h]) for h in hs]
    v_bf = [t.astype(bf16) for t in v_new]
    o = [_mm(qg[h], s_bf[h]) + _mm(qk[h], v_bf[h]) for h in hs]
    for h in hs:
        s_scr[h] = s_prev[h] * e_last[:, h:h + 1] + _mm_tn(kdec[h], v_bf[h])
    for h in hs:
        zh = proj_ref[0, :, z_off + h * hd:z_off + (h + 1) * hd].astype(f32)
        o_ref[0, :, h * hd:(h + 1) * hd] = (_rms(o[h][:rows], og_ref[...]) * (zh * _sigmoid(zh))).astype(o_ref.dtype)

    @pl.when(n == pl.num_programs(1) - 1)
    def _():
        s_out_ref[0] = s_scr[...]


def gdn_core(proj, conv_w, conv0, s0, dt_bias, a_log, onorm_g, out_dtype):
    b, l, p = proj.shape
    heads = s0.shape[1]
    hd = GDN_HEAD_DIM
    conv_dim = 3 * heads * hd
    rows = min(l, GDN_CHUNK_ROWS)
    chunk = GDN_CHUNK_ROWS if rows == GDN_CHUNK_ROWS else max(GDN_SHORT_CHUNK_ROWS, rows)
    assert l % rows == 0 and (rows == chunk or l == rows) and chunk % SUBLANES == 0
    conv0p = jnp.concatenate([jnp.zeros((b, SUBLANES - (CONV_W - 1), conv_dim), f32), conv0.astype(f32)], axis=1)
    pad_l = lambda t: jnp.pad(t.astype(f32), (0, LANES - heads)).reshape(1, LANES)
    kern = functools.partial(_gdn_kernel, heads=heads, rows=rows, chunk=chunk)
    return pl.pallas_call(
        kern,
        grid=(b, l // rows),
        in_specs=[
            pl.BlockSpec((1, rows, p), lambda i, n: (i, n, 0)),
            pl.BlockSpec((CONV_W, conv_dim), lambda i, n: (0, 0)),
            pl.BlockSpec((1, SUBLANES, conv_dim), lambda i, n: (i, 0, 0)),
            pl.BlockSpec((1, heads, hd, hd), lambda i, n: (i, 0, 0, 0)),
            pl.BlockSpec((1, LANES), lambda i, n: (0, 0)),
            pl.BlockSpec((1, LANES), lambda i, n: (0, 0)),
            pl.BlockSpec((1, hd), lambda i, n: (0, 0)),
        ],
        out_specs=[
            pl.BlockSpec((1, rows, heads * hd), lambda i, n: (i, n, 0)),
            pl.BlockSpec((1, heads, hd, hd), lambda i, n: (i, 0, 0, 0)),
        ],
        out_shape=[
            jax.ShapeDtypeStruct((b, l, heads * hd), out_dtype),
            jax.ShapeDtypeStruct(s0.shape, s0.dtype),
        ],
        scratch_shapes=[
            pltpu.VMEM((heads, hd, hd), f32),
            pltpu.VMEM((chunk + SUBLANES, conv_dim), f32),
        ],
        compiler_params=_cparams("parallel", "arbitrary"),
        name="gdn_core",
    )(proj, conv_w, conv0p, s0, pad_l(dt_bias), pad_l(a_log), onorm_g.reshape(1, hd))


def _mla_kv_kernel(x_ref, gin_ref, wc_ref, wpa_ref, wpb_ref, gkv_ref, cos_ref, sin_ref,
                   c_ref, pe_ref, *attn_refs, tk):
    h = _rms(x_ref[...], gin_ref[...]).astype(bf16)
    ckv = _rms(jnp.dot(h, wc_ref[...], preferred_element_type=f32), gkv_ref[...])
    pa = jnp.dot(h, wpa_ref[...], preferred_element_type=f32)
    pb = jnp.dot(h, wpb_ref[...], preferred_element_type=f32)
    kpe = pa * cos_ref[...] + pb * sin_ref[...]
    c_ref[...] = ckv
    pe_ref[...] = kpe
    if attn_refs:
        k_ref, ct_ref = attn_refs
        k_ref[...] = jnp.concatenate([ckv, kpe], axis=1).astype(bf16)
        for s in range(ckv.shape[0] // tk):
            ct_ref[s] = ckv[s * tk:(s + 1) * tk].T.astype(bf16)


def _swap_halves(w):
    half = w.shape[-1] // 2
    return jnp.concatenate([w[..., half:], w[..., :half]], axis=-1)


def mla_shared_kv(x, g_in, w_dkv, g_kv, cos2, sin2, attn_tk=None):
    m, d = x.shape
    tm = _row_tile(cos2.shape[0], 512)
    nt = cos2.shape[0] // tm
    w_c = w_dkv[:, :KV_LORA].astype(bf16)
    w_pa = w_dkv[:, KV_LORA:].astype(bf16)
    w_pb = _swap_halves(w_pa)
    full = lambda r, c: pl.BlockSpec((r, c), lambda i: (0, 0))
    rows = lambda c: pl.BlockSpec((tm, c), lambda i: (i, 0))
    tab = pl.BlockSpec((tm, ROPE_DIM), lambda i: (i % nt, 0))
    out_specs = [rows(KV_LORA), rows(ROPE_DIM)]
    out_shape = [jax.ShapeDtypeStruct((m, KV_LORA), f32), jax.ShapeDtypeStruct((m, ROPE_DIM), f32)]
    if attn_tk is not None:
        assert tm % attn_tk == 0
        out_specs += [rows(KV_LORA + ROPE_DIM),
                      pl.BlockSpec((tm // attn_tk, KV_LORA, attn_tk), lambda i: (i, 0, 0))]
        out_shape += [jax.ShapeDtypeStruct((m, KV_LORA + ROPE_DIM), bf16),
                      jax.ShapeDtypeStruct((m // attn_tk, KV_LORA, attn_tk), bf16)]
    return pl.pallas_call(
        functools.partial(_mla_kv_kernel, tk=attn_tk),
        grid=(m // tm,),
        in_specs=[rows(d), full(1, d), full(d, KV_LORA), full(d, ROPE_DIM), full(d, ROPE_DIM),
                  full(1, KV_LORA), tab, tab],
        out_specs=out_specs,
        out_shape=out_shape,
        compiler_params=_cparams("parallel"),
        name="mla_shared_kv",
    )(x, g_in.reshape(1, d), w_c, w_pa, w_pb, g_kv.reshape(1, KV_LORA), cos2, sin2)


def _mla_q_kernel(x_ref, gpre_ref, wdq_ref, gq_ref, wn_ref, wpa_ref, wpb_ref, wuk_ref, cos_ref, sin_ref,
                  q_ref, *, heads, scale):
    h = _rms(x_ref[...], gpre_ref[...]).astype(bf16)
    qa = _rms(jnp.dot(h, wdq_ref[...], preferred_element_type=f32), gq_ref[...]).astype(bf16)
    nope = jnp.dot(qa, wn_ref[...], preferred_element_type=f32).astype(bf16)
    pa = jnp.dot(qa, wpa_ref[...], preferred_element_type=f32)
    pb = jnp.dot(qa, wpb_ref[...], preferred_element_type=f32)
    qpe = (pa * cos_ref[...] + pb * sin_ref[...]) * scale
    for hh in range(heads):
        qlat = jnp.dot(nope[:, hh * NOPE_DIM:(hh + 1) * NOPE_DIM], wuk_ref[hh], preferred_element_type=f32)
        q_ref[hh] = jnp.concatenate([qlat * scale, qpe[:, hh * ROPE_DIM:(hh + 1) * ROPE_DIM]],
                                    axis=1).astype(q_ref.dtype)


def mla_query(x, g_pre, w_dq, g_q, w_uq, w_uk, cos2h, sin2h, out_dtype):
    m, d = x.shape
    heads = MLA_HEADS
    q_lora = w_dq.shape[1]
    qk_head = NOPE_DIM + ROPE_DIM
    tm = _row_tile(cos2h.shape[0], 512)
    nt = cos2h.shape[0] // tm
    w3 = w_uq.reshape(q_lora, heads, qk_head)
    w_n = w3[:, :, :NOPE_DIM].reshape(q_lora, heads * NOPE_DIM).astype(bf16)
    w_pa3 = w3[:, :, NOPE_DIM:]
    w_pa = w_pa3.reshape(q_lora, heads * ROPE_DIM).astype(bf16)
    w_pb = _swap_halves(w_pa3).reshape(q_lora, heads * ROPE_DIM).astype(bf16)
    w_ukt = jnp.transpose(w_uk.reshape(KV_LORA, heads, NOPE_DIM), (1, 2, 0)).astype(bf16)
    scale = qk_head ** -0.5
    full = lambda *s: pl.BlockSpec(s, lambda i: (0,) * len(s))
    tab = pl.BlockSpec((tm, heads * ROPE_DIM), lambda i: (i % nt, 0))
    kern = functools.partial(_mla_q_kernel, heads=heads, scale=scale)
    return pl.pallas_call(
        kern,
        grid=(m // tm,),
        in_specs=[pl.BlockSpec((tm, d), lambda i: (i, 0)), full(1, d), full(d, q_lora), full(1, q_lora),
                  full(q_lora, heads * NOPE_DIM), full(q_lora, heads * ROPE_DIM), full(q_lora, heads * ROPE_DIM),
                  full(heads, NOPE_DIM, KV_LORA), tab, tab],
        out_specs=pl.BlockSpec((heads, tm, KV_LORA + ROPE_DIM), lambda i: (0, i, 0)),
        out_shape=jax.ShapeDtypeStruct((heads, m, KV_LORA + ROPE_DIM), out_dtype),
        compiler_params=_cparams("parallel"),
        name="mla_query",
    )(x, g_pre.reshape(1, d), w_dq.astype(bf16), g_q.reshape(1, q_lora), w_n, w_pa, w_pb, w_ukt, cos2h, sin2h)


def _softmax_update(s, pv_fn, m_scr, l_scr, acc_scr):
    m_prev = m_scr[...]
    m_new = jnp.maximum(m_prev, jnp.max(s, axis=-1, keepdims=True))
    alpha = jnp.exp(m_prev - m_new)
    p = jnp.exp(s - m_new[:, 0:1])
    l_scr[...] = alpha * l_scr[...] + jnp.sum(p, axis=-1, keepdims=True)
    acc_scr[...] = acc_scr[...] * alpha[:, 0:1] + pv_fn(p.astype(bf16))
    m_scr[...] = m_new


def _softmax_init(m_scr, l_scr, acc_scr):
    m_scr[...] = jnp.full_like(m_scr, NEG_BIG)
    l_scr[...] = jnp.zeros_like(l_scr)
    acc_scr[...] = jnp.zeros_like(acc_scr)


def _prompt_attn_kernel(q_ref, k_ref, kt_ref, wuvt_ref, o_ref, acc_scr, *, heads, qb):
    i = pl.program_id(1)
    tk = qb
    hs = range(heads)
    k_idx = lax.broadcasted_iota(jnp.int32, (tk, qb), 0)
    q_idx = lax.broadcasted_iota(jnp.int32, (tk, qb), 1)
    causal = k_idx <= q_idx
    qs = [q_ref[h] for h in hs]

    def block(j, stats, masked):
        kj = k_ref[pl.ds(pl.multiple_of(j * tk, tk), tk), :]
        ktj = kt_ref[j]
        sts = [_mm_nt(kj, qs[h]) for h in hs]
        new = []
        for h in hs:
            st = jnp.where(causal, sts[h], NEG_BIG) if masked else sts[h]
            m_prev, l_prev = stats[h]
            m_new = jnp.maximum(m_prev, jnp.max(st, axis=0, keepdims=True))
            alpha = jnp.exp(m_prev - m_new)
            p = jnp.exp(st - m_new)
            l_new = alpha * l_prev + jnp.sum(p, axis=0, keepdims=True)
            acc_scr[h] = acc_scr[h] * alpha + jnp.dot(ktj, p.astype(bf16), preferred_element_type=f32)
            new.append((m_new, l_new))
        return tuple(new)

    acc_scr[...] = jnp.zeros_like(acc_scr)
    stats0 = tuple((jnp.full((1, qb), NEG_BIG, f32), jnp.zeros((1, qb), f32)) for _ in hs)
    stats = lax.fori_loop(0, i, lambda j, c: block(j, c, False), stats0)
    stats = block(i, stats, True)
    for h in hs:
        o_t = (acc_scr[h] / stats[h][1]).astype(bf16)
        head_t = jnp.dot(wuvt_ref[h], o_t, preferred_element_type=f32)
        o_ref[:, h * V_HEAD_DIM:(h + 1) * V_HEAD_DIM] = head_t.T.astype(o_ref.dtype)


def prompt_attention(q, keys, kt, w_uvt, batch, seq):
    heads = q.shape[0]
    qk = q.shape[2]
    qb = kt.shape[2]
    nq = seq // qb
    kern = functools.partial(_prompt_attn_kernel, heads=heads, qb=qb)
    return pl.pallas_call(
        kern,
        grid=(batch, nq),
        in_specs=[
            pl.BlockSpec((heads, qb, qk), lambda b, i: (0, b * nq + i, 0)),
            pl.BlockSpec((seq, qk), lambda b, i: (b, 0)),
            pl.BlockSpec((nq, KV_LORA, qb), lambda b, i: (b, 0, 0)),
            pl.BlockSpec((heads, V_HEAD_DIM, KV_LORA), lambda b, i: (0, 0, 0)),
        ],
        out_specs=pl.BlockSpec((qb, heads * V_HEAD_DIM), lambda b, i: (b * nq + i, 0)),
        out_shape=jax.ShapeDtypeStruct((batch * seq, heads * V_HEAD_DIM), bf16),
        scratch_shapes=[pltpu.VMEM((heads, KV_LORA, qb), f32)],
        compiler_params=_cparams("parallel", "arbitrary"),
        name="prompt_attention",
    )(q, keys, kt, w_uvt)


DECODE_PAGES_PER_CHUNK = 32
DECODE_SLOTS = 4
DECODE_SUBCHAINS = 2


def _decode_attn_kernel(pt_ref, q_ref, cnew_ref, penew_ref, wuv_ref, ck_hbm, pe_hbm, o_ref,
                        kbuf, pbuf, ksem, psem, m_scr, l_scr, acc_scr, *, heads, seq, n_chunks, n_rows, pages):
    b = pl.program_id(0)
    r = heads * seq
    ahead = DECODE_SLOTS - 1
    sub = pages // DECODE_SUBCHAINS
    q_all = q_ref[...].reshape(r, KV_LORA + ROPE_DIM).astype(bf16)
    q = q_all[:, :KV_LORA]
    qp = q_all[:, KV_LORA:]

    def chunk_copies(row, ch, slot):
        out = []
        for p in range(pages):
            page = pt_ref[row, ch * pages + p]
            out.append(pltpu.make_async_copy(ck_hbm.at[page], kbuf.at[slot, p], ksem.at[slot]))
            out.append(pltpu.make_async_copy(pe_hbm.at[page], pbuf.at[slot, p], psem.at[slot]))
        return out

    def start_chunk(row, ch):
        @pl.when(row < n_rows)
        def _():
            slot = (row * n_chunks + ch) & (DECODE_SLOTS - 1)
            for cp in chunk_copies(row, ch, slot):
                cp.start()

    def start_ahead(ch):
        wrap = ch + ahead >= n_chunks
        start_chunk(jnp.where(wrap, b + 1, b), jnp.where(wrap, ch + ahead - n_chunks, ch + ahead))

    @pl.when(b == 0)
    def _():
        for t in range(min(ahead, n_chunks * n_rows)):
            start_chunk(jnp.int32(t // n_chunks), jnp.int32(t % n_chunks))

    for g in range(DECODE_SUBCHAINS):
        _softmax_init(m_scr.at[g], l_scr.at[g], acc_scr.at[g])

    def consume(ch, carry):
        slot = (b * n_chunks + ch) & (DECODE_SLOTS - 1)
        for cp in chunk_copies(b, ch, slot):
            cp.wait()
        start_ahead(ch)
        for g in range(DECODE_SUBCHAINS):
            kc = kbuf[slot, g * sub:(g + 1) * sub].reshape(sub * PAGE_SIZE, KV_LORA).astype(bf16)
            kp_t = jnp.concatenate([pbuf[slot, p] for p in range(g * sub, (g + 1) * sub)], axis=1)
            s = _mm_nt(q, kc) + _mm(qp, kp_t)
            _softmax_update(s, lambda pm, kc=kc: jnp.dot(pm, kc, preferred_element_type=f32),
                            m_scr.at[g], l_scr.at[g], acc_scr.at[g])
        return carry

    lax.fori_loop(0, n_chunks, consume, 0)

    kc = jnp.concatenate([cnew_ref[0], jnp.zeros((PAGE_SIZE - seq, KV_LORA), f32)], axis=0).astype(bf16)
    kp = jnp.concatenate([penew_ref[0], jnp.zeros((PAGE_SIZE - seq, ROPE_DIM), f32)], axis=0)
    s = _mm_nt(q, kc) + _mm_nt(qp, kp)
    q_t = lax.rem(lax.broadcasted_iota(jnp.int32, (r, PAGE_SIZE), 0), jnp.int32(seq))
    k_t = lax.broadcasted_iota(jnp.int32, (r, PAGE_SIZE), 1)
    s = jnp.where(k_t <= q_t, s, NEG_BIG)
    _softmax_update(s, lambda pm: jnp.dot(pm, kc, preferred_element_type=f32), m_scr.at[0], l_scr.at[0], acc_scr.at[0])
    m_all = m_scr[0]
    for g in range(1, DECODE_SUBCHAINS):
        m_all = jnp.maximum(m_all, m_scr[g])
    l_all = jnp.zeros_like(m_all)
    acc_all = jnp.zeros((r, KV_LORA), f32)
    for g in range(DECODE_SUBCHAINS):
        w = jnp.exp(m_scr[g] - m_all)
        l_all = l_all + w * l_scr[g]
        acc_all = acc_all + w[:, 0:1] * acc_scr[g]
    o_lat = (acc_all / l_all[:, 0:1]).astype(bf16)
    for hh in range(heads):
        full = jnp.dot(o_lat, wuv_ref[hh], preferred_element_type=f32)
        o_ref[:, hh * V_HEAD_DIM:(hh + 1) * V_HEAD_DIM] = full[hh * seq:(hh + 1) * seq]


def decode_attention(q, c_new, pe_new, cache_ckv, cache_kpe_t, page_table, w_uv3):
    heads = q.shape[0]
    db, seq, _ = c_new.shape
    n_pages = page_table.shape[1]
    pages = DECODE_PAGES_PER_CHUNK
    assert n_pages % pages == 0 and pages % DECODE_SUBCHAINS == 0
    n_chunks = n_pages // pages
    assert n_chunks >= DECODE_SLOTS - 1
    r = heads * seq
    kern = functools.partial(_decode_attn_kernel, heads=heads, seq=seq, n_chunks=n_chunks, n_rows=db, pages=pages)
    grid_spec = pltpu.PrefetchScalarGridSpec(
        num_scalar_prefetch=1,
        grid=(db,),
        in_specs=[
            pl.BlockSpec((heads, seq, KV_LORA + ROPE_DIM), lambda b, pt: (0, b, 0)),
            pl.BlockSpec((1, seq, KV_LORA), lambda b, pt: (b, 0, 0)),
            pl.BlockSpec((1, seq, ROPE_DIM), lambda b, pt: (b, 0, 0)),
            pl.BlockSpec((heads, KV_LORA, V_HEAD_DIM), lambda b, pt: (0, 0, 0)),
            pl.BlockSpec(memory_space=pl.ANY),
            pl.BlockSpec(memory_space=pl.ANY),
        ],
        out_specs=pl.BlockSpec((seq, heads * V_HEAD_DIM), lambda b, pt: (b, 0)),
        scratch_shapes=[
            pltpu.VMEM((DECODE_SLOTS, pages, PAGE_SIZE, KV_LORA), f32),
            pltpu.VMEM((DECODE_SLOTS, pages, ROPE_DIM, PAGE_SIZE), f32),
            pltpu.SemaphoreType.DMA((DECODE_SLOTS,)),
            pltpu.SemaphoreType.DMA((DECODE_SLOTS,)),
            pltpu.VMEM((DECODE_SUBCHAINS, r, LANES), f32),
            pltpu.VMEM((DECODE_SUBCHAINS, r, LANES), f32),
            pltpu.VMEM((DECODE_SUBCHAINS, r, KV_LORA), f32),
        ],
    )
    return pl.pallas_call(
        kern,
        grid_spec=grid_spec,
        out_shape=jax.ShapeDtypeStruct((db * seq, heads * V_HEAD_DIM), f32),
        compiler_params=_cparams("arbitrary"),
        name="decode_attention",
    )(page_table, q, c_new, pe_new, w_uv3, cache_ckv, cache_kpe_t)


def _rope_tables(first_pos, seq):
    half = ROPE_DIM // 2
    inv_freq = ROPE_THETA ** (-jnp.arange(half, dtype=f32) / half)
    pos = first_pos + jnp.arange(seq, dtype=jnp.int32)
    ang = pos.astype(f32)[:, None] * inv_freq[None, :]
    cos, sin = jnp.cos(ang), jnp.sin(ang)
    return jnp.concatenate([cos, cos], axis=1), jnp.concatenate([-sin, sin], axis=1)


def _run_group(x3, gdn_s0, conv0, past, wts):
    batch, seq, d = x3.shape
    n_a = wts["gdn_w_in"].shape[0]
    depth = wts["ffn_in_b"].shape[0]
    x = x3.reshape(batch * seq, d)
    past_len = 0 if past is None else past[2].shape[1] * PAGE_SIZE
    cos2, sin2 = _rope_tables(past_len, seq)
    if seq < LANES:
        cos2, sin2 = jnp.tile(cos2, (batch, 1)), jnp.tile(sin2, (batch, 1))
    cos2h, sin2h = jnp.tile(cos2, (1, MLA_HEADS)), jnp.tile(sin2, (1, MLA_HEADS))
    act_dtype = bf16 if past is None else f32
    states, convs = [], []
    c_new = pe_new = keys_bf = kt_bf = None
    attn_tk = _row_tile(seq, 256) if past is None else None
    for layer in range(depth):
        if layer == n_a:
            kv = mla_shared_kv(x, wts["mla_kvin_norm"], wts["mla_w_dkv"], wts["mla_kv_norm"], cos2, sin2, attn_tk)
            c_new, pe_new = kv[0], kv[1]
            if past is None:
                keys_bf, kt_bf = kv[2], kv[3]
        x = ffn_sublayer(x, wts["norm_pre"][layer, 0], wts["ffn_in_b"], wts["ffn_out_b"],
                         wts["norm_post"][layer, 0], layer, 0)
        if layer < n_a:
            proj = norm_proj(x, wts["norm_pre"][layer, 1], wts["gdn_w_in_b"][layer], act_dtype)
            proj3 = proj.reshape(batch, seq, -1)
            conv_dim = conv0.shape[-1]
            o, s_new = gdn_core(proj3, wts["gdn_conv_w"][layer], conv0[layer], gdn_s0[layer],
                                wts["gdn_dt_bias"][layer], wts["gdn_a_log"][layer], wts["gdn_norm"][layer], act_dtype)
            states.append(s_new)
            if seq >= CONV_W - 1:
                convs.append(proj3[:, seq - (CONV_W - 1):, :conv_dim].astype(conv0.dtype))
            else:
                prev = jnp.concatenate([conv0[layer].astype(f32), proj3[:, :, :conv_dim]], axis=1)
                convs.append(prev[:, -(CONV_W - 1):])
            mixer_out = (o.reshape(batch * seq, -1), wts["gdn_w_out_b"][layer], wts["norm_post"][layer, 1])
        else:
            jb = layer - n_a
            q = mla_query(x, wts["norm_pre"][layer, 1], wts["mla_w_dq"][jb], wts["mla_q_norm"][jb],
                                    wts["mla_w_uq"][jb], wts["mla_w_uk"], cos2h, sin2h, act_dtype)
            if past is None:
                o = prompt_attention(q, keys_bf, kt_bf, wts["w_uvt"], batch, seq)
            else:
                o = decode_attention(q, c_new.reshape(batch, seq, -1), pe_new.reshape(batch, seq, -1),
                                     past[0], past[1], past[2], wts["w_uv3"])
            mixer_out = (o, wts["mla_w_o_b"][jb], wts["norm_post"][layer, 1])
        x = ffn_sublayer(x, wts["norm_pre"][layer, 2], wts["ffn_in_b"], wts["ffn_out_b"],
                         wts["norm_post"][layer, 2], layer, 1, mixer_out)
    return (x.reshape(batch, seq, d), jnp.stack(states), jnp.stack(convs),
            c_new.reshape(batch, seq, -1), pe_new.reshape(batch, seq, -1))


def _pack_gdn_w_in(w_in, heads):
    main = 4 * heads * GDN_HEAD_DIM
    padw = ((0, 0), (0, 0), (0, LANES - heads))
    b = jnp.pad(w_in[..., main:main + heads], padw)
    a = jnp.pad(w_in[..., main + heads:main + 2 * heads], padw)
    return jnp.concatenate([w_in[..., :main], b, a], axis=-1).astype(bf16)


def kernel(x_prompt, x_sample, state_gdn, state_conv, cache_ckv, cache_kpe, page_table, norm_pre, norm_post, ffn_in, ffn_out, gdn_w_in, gdn_conv_w, gdn_dt_bias, gdn_a_log, gdn_norm, gdn_w_out, mla_kvin_norm, mla_w_dkv, mla_kv_norm, mla_w_uk, mla_w_uv, mla_w_dq, mla_q_norm, mla_w_uq, mla_w_o):
    n_a = gdn_w_in.shape[0]
    b = x_prompt.shape[0]
    heads = state_gdn.shape[2]
    wts = dict(
        norm_pre=norm_pre, norm_post=norm_post,
        ffn_in_b=ffn_in.astype(bf16), ffn_out_b=ffn_out.astype(bf16),
        gdn_w_in=gdn_w_in, gdn_w_in_b=_pack_gdn_w_in(gdn_w_in, heads), gdn_conv_w=gdn_conv_w,
        gdn_dt_bias=gdn_dt_bias, gdn_a_log=gdn_a_log, gdn_norm=gdn_norm, gdn_w_out_b=gdn_w_out.astype(bf16),
        mla_kvin_norm=mla_kvin_norm, mla_w_dkv=mla_w_dkv, mla_kv_norm=mla_kv_norm, mla_w_uk=mla_w_uk,
        w_uv3=jnp.transpose(mla_w_uv.reshape(KV_LORA, MLA_HEADS, V_HEAD_DIM), (1, 0, 2)).astype(bf16),
        w_uvt=jnp.transpose(mla_w_uv.reshape(KV_LORA, MLA_HEADS, V_HEAD_DIM), (1, 2, 0)).astype(bf16),
        mla_w_dq=mla_w_dq, mla_q_norm=mla_q_norm, mla_w_uq=mla_w_uq, mla_w_o_b=mla_w_o.astype(bf16),
    )
    zero_s = jnp.zeros((n_a, b) + state_gdn.shape[2:], state_gdn.dtype)
    zero_conv = jnp.zeros((n_a, b) + state_conv.shape[2:], state_conv.dtype)
    y_p, sg_p, cv_p, ckv_p, kpe_p = _run_group(x_prompt, zero_s, zero_conv, None, wts)
    y_s, sg_s, cv_s, ckv_s, kpe_s = _run_group(x_sample, state_gdn, state_conv,
                                               (cache_ckv, jnp.swapaxes(cache_kpe, 1, 2), page_table), wts)
    return (y_p, y_s, sg_p, cv_p, ckv_p, kpe_p, sg_s, cv_s, ckv_s, kpe_s)
```

```python
import functools
import math

import jax
import jax.numpy as jnp
from jax import lax
from jax.experimental import pallas as pl
from jax.experimental.pallas import tpu as pltpu

EPS = 1e-6
ROPE_THETA = 10000.0
GDN_HEADS = 8
GDN_HEAD_DIM = 128
CONV_W = 4
MLA_HEADS = 8
KV_LORA = 256
NOPE_DIM = 128
ROPE_DIM = 64
V_HEAD_DIM = 128
PAGE_SIZE = 128

LANES = 128
SUBLANES = 8
VMEM_LIMIT_BYTES = 56 * 1024 * 1024
GDN_CHUNK_ROWS = 128
GDN_SHORT_CHUNK_ROWS = 16
FFN_ROW_TILE = 1024
FFN_ROW_SPLIT = 2
FFN_CHUNK = 512
NEG_BIG = -1e30
LOG2_E = math.log2(math.e)

bf16 = jnp.bfloat16
f32 = jnp.float32


def _cparams(*sem):
    return pltpu.CompilerParams(dimension_semantics=sem, vmem_limit_bytes=VMEM_LIMIT_BYTES)


def _rms(x, g):
    return x * lax.rsqrt(jnp.mean(x * x, axis=-1, keepdims=True) + EPS) * g


def _sigmoid(x):
    return 1.0 / (1.0 + jnp.exp2(x * -LOG2_E))


def _mm(a, b):
    return jnp.dot(a.astype(bf16), b.astype(bf16), preferred_element_type=f32)


def _mm_nt(a, b):
    return lax.dot_general(a.astype(bf16), b.astype(bf16), (((1,), (1,)), ((), ())),
                           preferred_element_type=f32)


def _mm_tn(a, b):
    return lax.dot_general(a.astype(bf16), b.astype(bf16), (((0,), (0,)), ((), ())),
                           preferred_element_type=f32)


def _gain_spec(gain):
    stacked, row = gain
    return pl.BlockSpec((None, 1, stacked.shape[2]), lambda *_: (row, 0, 0))


def _row_tile(m, cap):
    t = min(m, cap)
    while m % t:
        t //= 2
    return t


def _ffn_kernel(x_ref, gpre_ref, win_ref, wout_ref, gpost_ref, *rest, chunks):
    o_ref = rest[-1]
    f = wout_ref.shape[0]
    tm = x_ref.shape[0]
    n_split = FFN_ROW_SPLIT if tm % (FFN_ROW_SPLIT * SUBLANES * 2) == 0 else 1
    rows = [(i * tm // n_split, (i + 1) * tm // n_split) for i in range(n_split)]
    xs = [x_ref[r0:r1, :] for r0, r1 in rows]
    if len(rest) > 1:
        a_ref, wmix_ref, gmix_ref = rest[:3]
        xs = [x + _rms(jnp.dot(a_ref[r0:r1, :].astype(bf16), wmix_ref[...], preferred_element_type=f32), gmix_ref[...])
              for x, (r0, r1) in zip(xs, rows)]
    hs = [_rms(x, gpre_ref[...]).astype(bf16) for x in xs]
    accs = [None] * n_split
    for c0, c1 in chunks:
        for i in range(n_split):
            g = jnp.dot(hs[i], win_ref[:, c0:c1], preferred_element_type=f32)
            u = jnp.dot(hs[i], win_ref[:, f + c0:f + c1], preferred_element_type=f32)
            a = (g * _sigmoid(g) * u).astype(bf16)
            part = jnp.dot(a, wout_ref[c0:c1, :], preferred_element_type=f32)
            accs[i] = part if accs[i] is None else accs[i] + part
    for x, acc, (r0, r1) in zip(xs, accs, rows):
        o_ref[r0:r1, :] = x + 0.5 * _rms(acc, gpost_ref[...])


def _ffn_chunks(f, width):
    assert f % LANES == 0
    return tuple((c0, min(c0 + width, f)) for c0 in range(0, f, width))


def ffn_sublayer(x, g_pre, w_in, w_out, g_post, layer, k, mixer_out=None):
    m, d = x.shape
    f = w_out.shape[2]
    tm = _row_tile(m, FFN_ROW_TILE)
    resident = pl.Buffered(1)
    mix_specs, mix_args = [], []
    if mixer_out is not None:
        a, w_mix, g_mix = mixer_out
        kdim = a.shape[1]
        mix_specs = [pl.BlockSpec((tm, kdim), lambda i: (i, 0)),
                     pl.BlockSpec((kdim, d), lambda i: (0, 0), pipeline_mode=resident),
                     _gain_spec(g_mix)]
        mix_args = [a, w_mix, g_mix[0]]
    return pl.pallas_call(
        functools.partial(_ffn_kernel, chunks=_ffn_chunks(f, FFN_CHUNK)),
        grid=(m // tm,),
        in_specs=[
            pl.BlockSpec((tm, d), lambda i: (i, 0)),
            _gain_spec(g_pre),
            pl.BlockSpec((None, None, d, 2 * f), lambda i: (layer, k, 0, 0), pipeline_mode=resident),
            pl.BlockSpec((None, None, f, d), lambda i: (layer, k, 0, 0), pipeline_mode=resident),
            _gain_spec(g_post),
        ] + mix_specs,
        out_specs=pl.BlockSpec((tm, d), lambda i: (i, 0)),
        out_shape=jax.ShapeDtypeStruct((m, d), f32),
        compiler_params=_cparams("parallel"),
        name="ffn_sublayer",
    )(x, g_pre[0], w_in, w_out, g_post[0], *mix_args)


def _norm_proj_kernel(x_ref, g_ref, w_ref, o_ref, h_scr):
    @pl.when(pl.program_id(1) == 0)
    def _():
        h_scr[...] = _rms(x_ref[...], g_ref[...]).astype(bf16)

    o_ref[...] = jnp.dot(h_scr[...], w_ref[...], preferred_element_type=f32).astype(o_ref.dtype)


def norm_proj(x, g, w, layer, out_dtype):
    m, d = x.shape
    n = w.shape[2]
    tm = _row_tile(m, 1024)
    tn = n // 2 if (n // 2) % LANES == 0 and n > 2304 else n
    return pl.pallas_call(
        _norm_proj_kernel,
        grid=(m // tm, n // tn),
        in_specs=[
            pl.BlockSpec((tm, d), lambda i, j: (i, 0)),
            _gain_spec(g),
            pl.BlockSpec((None, d, tn), lambda i, j: (layer, 0, j)),
        ],
        out_specs=pl.BlockSpec((tm, tn), lambda i, j: (i, j)),
        out_shape=jax.ShapeDtypeStruct((m, n), out_dtype),
        scratch_shapes=[pltpu.VMEM((tm, d), bf16)],
        compiler_params=_cparams("parallel", "arbitrary"),
        name="gdn_in_proj",
    )(x, g[0], w)


def _gdn_kernel(proj_ref, convw_ref, conv0_ref, s0_ref, dtb_ref, alog_ref, og_ref,
                o_ref, s_out_ref, s_scr, xp_scr, *, heads, rows, chunk):
    n = pl.program_id(1)
    hd = GDN_HEAD_DIM
    c = chunk
    qk_dim = heads * hd
    conv_dim = 3 * qk_dim
    z_off = conv_dim
    b_off = z_off + qk_dim
    a_off = b_off + LANES
    pad = SUBLANES

    @pl.when(n == 0)
    def _():
        s_scr[...] = s0_ref[0]
        xp_scr[0:pad, :] = conv0_ref[0]
        if rows < c:
            xp_scr[pad + rows:pad + c, :] = jnp.zeros((c - rows, conv_dim), f32)

    if rows == c:
        @pl.when(n > 0)
        def _():
            xp_scr[0:pad, :] = xp_scr[c:c + pad, :]

    xp_scr[pad:pad + rows, :] = proj_ref[0, :, 0:conv_dim].astype(f32)

    def gate_block(off):
        blk = proj_ref[0, :, off:off + LANES].astype(f32)
        if rows < c:
            blk = jnp.concatenate([blk, jnp.zeros((c - rows, LANES), f32)], axis=0)
        return blk

    row_i = lax.broadcasted_iota(jnp.int32, (c, c), 0)
    col_i = lax.broadcasted_iota(jnp.int32, (c, c), 1)
    tril = row_i >= col_i
    strict = row_i > col_i
    real = lax.broadcasted_iota(jnp.int32, (c, LANES), 0) < rows

    beta = jnp.where(real, _sigmoid(gate_block(b_off)), 0.0)
    a_in = gate_block(a_off) + dtb_ref[...]
    softplus = jnp.maximum(a_in, 0.0) + jnp.log(1.0 + jnp.exp(-jnp.abs(a_in)))
    g = jnp.where(real, -jnp.exp(alog_ref[...]) * softplus, 0.0)
    g_hi = g.astype(bf16)
    g_mid = (g - g_hi.astype(f32)).astype(bf16)
    g_lo = (g - g_hi.astype(f32) - g_mid.astype(f32)).astype(bf16)
    gsum = jnp.dot(jnp.where(tril, 1.0, 0.0).astype(bf16), jnp.concatenate([g_hi, g_mid, g_lo], axis=1),
                   preferred_element_type=f32)
    gcum = gsum[:, :LANES] + gsum[:, LANES:2 * LANES] + gsum[:, 2 * LANES:]
    gcum_t = gcum.T
    glast = gcum[c - 1:c, :]
    e_g = jnp.exp(gcum)
    e_kd = jnp.exp(glast - gcum)
    e_last = jnp.exp(glast)

    def conv_silu(col0):
        acc = xp_scr[pad - 3:pad - 3 + c, col0:col0 + hd] * convw_ref[0:1, col0:col0 + hd]
        for j in range(1, CONV_W):
            acc = acc + xp_scr[pad - 3 + j:pad - 3 + j + c, col0:col0 + hd] * convw_ref[j:j + 1, col0:col0 + hd]
        return acc * _sigmoid(acc)

    def l2norm(t, scale=1.0):
        return t * (lax.rsqrt(jnp.sum(t * t, axis=-1, keepdims=True) + EPS) * scale)

    def col(t, h):
        return jnp.broadcast_to(t[:, h:h + 1], (t.shape[0], hd))

    n_iter = max(int(math.ceil(math.log2(rows))) - 1, 0)

    hs = range(heads)
    q = [l2norm(conv_silu(h * hd), hd ** -0.5) for h in hs]
    k = [l2norm(conv_silu(qk_dim + h * hd)) for h in hs]
    v = [conv_silu(2 * qk_dim + h * hd) for h in hs]
    beta_c = [col(beta, h) for h in hs]
    e_g_c = [col(e_g, h) for h in hs]
    decay = []
    for h in hs:
        diff = jnp.broadcast_to(gcum[:, h:h + 1], (c, c)) - jnp.broadcast_to(gcum_t[h:h + 1, :], (c, c))
        decay.append(jnp.exp(jnp.where(tril, diff, NEG_BIG)))
    kb = [k[h] * beta_c[h] for h in hs]
    k_bf = [k[h].astype(bf16) for h in hs]
    pw = [-jnp.where(strict, _mm_nt(kb[h], k_bf[h]) * decay[h], 0.0) for h in hs]
    nm = list(pw)
    for _ in range(n_iter):
        pw_bf = [p.astype(bf16) for p in pw]
        pw = [_mm(pw_bf[h], pw_bf[h]) for h in hs]
        pw_bf = [p.astype(bf16) for p in pw]
        nm = [nm[h] + pw[h] + _mm(nm[h], pw_bf[h]) for h in hs]
    rhs = [jnp.concatenate([v[h] * beta_c[h], kb[h] * e_g_c[h]], axis=1) for h in hs]
    sol = [rhs[h] + _mm(nm[h], rhs[h]) for h in hs]
    qk = [_mm_nt(q[h], k_bf[h]) * decay[h] for h in hs]
    qg = [q[h] * e_g_c[h] for h in hs]
    kdec = [k[h] * col(e_kd, h) for h in hs]
    s_prev = [s_scr[h] for h in hs]
    s_bf = [s.astype(bf16) for s in s_prev]
    v_new = [sol[h][:, :hd] - _mm(sol[h][:, hd:], s_bf[h]) for h in hs]
    v_bf = [t.astype(bf16) for t in v_new]
    o = [_mm(qg[h], s_bf[h]) + _mm(qk[h], v_bf[h]) for h in hs]
    for h in hs:
        s_scr[h] = s_prev[h] * e_last[:, h:h + 1] + _mm_tn(kdec[h], v_bf[h])
    for h in hs:
        zh = proj_ref[0, :, z_off + h * hd:z_off + (h + 1) * hd].astype(f32)
        o_ref[0, :, h * hd:(h + 1) * hd] = (_rms(o[h][:rows], og_ref[...]) * (zh * _sigmoid(zh))).astype(o_ref.dtype)

    @pl.when(n == pl.num_programs(1) - 1)
    def _():
        s_out_ref[0] = s_scr[...]


def gdn_core(proj, conv_w, conv0, s0, dt_bias, a_log, onorm_g, out_dtype):
    b, l, p = proj.shape
    heads = s0.shape[1]
    hd = GDN_HEAD_DIM
    conv_dim = 3 * heads * hd
    rows = min(l, GDN_CHUNK_ROWS)
    chunk = GDN_CHUNK_ROWS if rows == GDN_CHUNK_ROWS else max(GDN_SHORT_CHUNK_ROWS, rows)
    assert l % rows == 0 and (rows == chunk or l == rows) and chunk % SUBLANES == 0
    conv0p = jnp.concatenate([jnp.zeros((b, SUBLANES - (CONV_W - 1), conv_dim), f32), conv0.astype(f32)], axis=1)
    pad_l = lambda t: jnp.pad(t.astype(f32), (0, LANES - heads)).reshape(1, LANES)
    kern = functools.partial(_gdn_kernel, heads=heads, rows=rows, chunk=chunk)
    return pl.pallas_call(
        kern,
        grid=(b, l // rows),
        in_specs=[
            pl.BlockSpec((1, rows, p), lambda i, n: (i, n, 0)),
            pl.BlockSpec((CONV_W, conv_dim), lambda i, n: (0, 0)),
            pl.BlockSpec((1, SUBLANES, conv_dim), lambda i, n: (i, 0, 0)),
            pl.BlockSpec((1, heads, hd, hd), lambda i, n: (i, 0, 0, 0)),
            pl.BlockSpec((1, LANES), lambda i, n: (0, 0)),
            pl.BlockSpec((1, LANES), lambda i, n: (0, 0)),
            pl.BlockSpec((1, hd), lambda i, n: (0, 0)),
        ],
        out_specs=[
            pl.BlockSpec((1, rows, heads * hd), lambda i, n: (i, n, 0)),
            pl.BlockSpec((1, heads, hd, hd), lambda i, n: (i, 0, 0, 0)),
        ],
        out_shape=[
            jax.ShapeDtypeStruct((b, l, heads * hd), out_dtype),
            jax.ShapeDtypeStruct(s0.shape, s0.dtype),
        ],
        scratch_shapes=[
            pltpu.VMEM((heads, hd, hd), f32),
            pltpu.VMEM((chunk + SUBLANES, conv_dim), f32),
        ],
        compiler_params=_cparams("parallel", "arbitrary"),
        name="gdn_core",
    )(proj, conv_w, conv0p, s0, pad_l(dt_bias), pad_l(a_log), onorm_g.reshape(1, hd))


def _mla_kv_kernel(x_ref, gin_ref, wc_ref, wpa_ref, wpb_ref, gkv_ref, cos_ref, sin_ref,
                   c_ref, pe_ref, *attn_refs, tk):
    h = _rms(x_ref[...], gin_ref[...]).astype(bf16)
    ckv = _rms(jnp.dot(h, wc_ref[...], preferred_element_type=f32), gkv_ref[...])
    pa = jnp.dot(h, wpa_ref[...], preferred_element_type=f32)
    pb = jnp.dot(h, wpb_ref[...], preferred_element_type=f32)
    kpe = pa * cos_ref[...] + pb * sin_ref[...]
    c_ref[...] = ckv
    pe_ref[...] = kpe
    if attn_refs:
        k_ref, ct_ref = attn_refs
        k_ref[...] = jnp.concatenate([ckv, kpe], axis=1).astype(bf16)
        for s in range(ckv.shape[0] // tk):
            ct_ref[s] = ckv[s * tk:(s + 1) * tk].T.astype(bf16)


def _swap_halves(w):
    half = w.shape[-1] // 2
    return jnp.concatenate([w[..., half:], w[..., :half]], axis=-1)


def mla_shared_kv(x, g_in, w_dkv, g_kv, cos2, sin2, attn_tk=None):
    m, d = x.shape
    tm = _row_tile(cos2.shape[0], 512)
    nt = cos2.shape[0] // tm
    w_c = w_dkv[:, :KV_LORA].astype(bf16)
    w_pa = w_dkv[:, KV_LORA:].astype(bf16)
    w_pb = _swap_halves(w_pa)
    full = lambda r, c: pl.BlockSpec((r, c), lambda i: (0, 0))
    rows = lambda c: pl.BlockSpec((tm, c), lambda i: (i, 0))
    tab = pl.BlockSpec((tm, ROPE_DIM), lambda i: (i % nt, 0))
    out_specs = [rows(KV_LORA), rows(ROPE_DIM)]
    out_shape = [jax.ShapeDtypeStruct((m, KV_LORA), f32), jax.ShapeDtypeStruct((m, ROPE_DIM), f32)]
    if attn_tk is not None:
        assert tm % attn_tk == 0
        out_specs += [rows(KV_LORA + ROPE_DIM),
                      pl.BlockSpec((tm // attn_tk, KV_LORA, attn_tk), lambda i: (i, 0, 0))]
        out_shape += [jax.ShapeDtypeStruct((m, KV_LORA + ROPE_DIM), bf16),
                      jax.ShapeDtypeStruct((m // attn_tk, KV_LORA, attn_tk), bf16)]
    return pl.pallas_call(
        functools.partial(_mla_kv_kernel, tk=attn_tk),
        grid=(m // tm,),
        in_specs=[rows(d), full(1, d), full(d, KV_LORA), full(d, ROPE_DIM), full(d, ROPE_DIM),
                  full(1, KV_LORA), tab, tab],
        out_specs=out_specs,
        out_shape=out_shape,
        compiler_params=_cparams("parallel"),
        name="mla_shared_kv",
    )(x, g_in.reshape(1, d), w_c, w_pa, w_pb, g_kv.reshape(1, KV_LORA), cos2, sin2)


def _mla_q_kernel(x_ref, gpre_ref, wdq_ref, gq_ref, wn_ref, wpa_ref, wpb_ref, wuk_ref, cos_ref, sin_ref,
                  q_ref, *, heads, scale):
    h = _rms(x_ref[...], gpre_ref[...]).astype(bf16)
    qa = _rms(jnp.dot(h, wdq_ref[...], preferred_element_type=f32), gq_ref[...]).astype(bf16)
    nope = jnp.dot(qa, wn_ref[...], preferred_element_type=f32).astype(bf16)
    pa = jnp.dot(qa, wpa_ref[...], preferred_element_type=f32)
    pb = jnp.dot(qa, wpb_ref[...], preferred_element_type=f32)
    qpe = (pa * cos_ref[...] + pb * sin_ref[...]) * scale
    for hh in range(heads):
        qlat = jnp.dot(nope[:, hh * NOPE_DIM:(hh + 1) * NOPE_DIM], wuk_ref[hh], preferred_element_type=f32)
        q_ref[hh] = jnp.concatenate([qlat * scale, qpe[:, hh * ROPE_DIM:(hh + 1) * ROPE_DIM]],
                                    axis=1).astype(q_ref.dtype)


def mla_query(x, g_pre, w_dq, g_q, w_uq, w_uk, cos2h, sin2h, out_dtype):
    m, d = x.shape
    heads = MLA_HEADS
    q_lora = w_dq.shape[1]
    qk_head = NOPE_DIM + ROPE_DIM
    tm = _row_tile(cos2h.shape[0], 512)
    nt = cos2h.shape[0] // tm
    w3 = w_uq.reshape(q_lora, heads, qk_head)
    w_n = w3[:, :, :NOPE_DIM].reshape(q_lora, heads * NOPE_DIM).astype(bf16)
    w_pa3 = w3[:, :, NOPE_DIM:]
    w_pa = w_pa3.reshape(q_lora, heads * ROPE_DIM).astype(bf16)
    w_pb = _swap_halves(w_pa3).reshape(q_lora, heads * ROPE_DIM).astype(bf16)
    w_ukt = jnp.transpose(w_uk.reshape(KV_LORA, heads, NOPE_DIM), (1, 2, 0)).astype(bf16)
    scale = qk_head ** -0.5
    full = lambda *s: pl.BlockSpec(s, lambda i: (0,) * len(s))
    tab = pl.BlockSpec((tm, heads * ROPE_DIM), lambda i: (i % nt, 0))
    kern = functools.partial(_mla_q_kernel, heads=heads, scale=scale)
    return pl.pallas_call(
        kern,
        grid=(m // tm,),
        in_specs=[pl.BlockSpec((tm, d), lambda i: (i, 0)), _gain_spec(g_pre), full(d, q_lora), full(1, q_lora),
                  full(q_lora, heads * NOPE_DIM), full(q_lora, heads * ROPE_DIM), full(q_lora, heads * ROPE_DIM),
                  full(heads, NOPE_DIM, KV_LORA), tab, tab],
        out_specs=pl.BlockSpec((heads, tm, KV_LORA + ROPE_DIM), lambda i: (0, i, 0)),
        out_shape=jax.ShapeDtypeStruct((heads, m, KV_LORA + ROPE_DIM), out_dtype),
        compiler_params=_cparams("parallel"),
        name="mla_query",
    )(x, g_pre[0], w_dq.astype(bf16), g_q.reshape(1, q_lora), w_n, w_pa, w_pb, w_ukt, cos2h, sin2h)


def _softmax_update(s, pv_fn, m_scr, l_scr, acc_scr):
    m_prev = m_scr[...]
    m_new = jnp.maximum(m_prev, jnp.max(s, axis=-1, keepdims=True))
    alpha = jnp.exp(m_prev - m_new)
    p = jnp.exp(s - m_new[:, 0:1])
    l_scr[...] = alpha * l_scr[...] + jnp.sum(p, axis=-1, keepdims=True)
    acc_scr[...] = acc_scr[...] * alpha[:, 0:1] + pv_fn(p.astype(bf16))
    m_scr[...] = m_new


def _softmax_init(m_scr, l_scr, acc_scr):
    m_scr[...] = jnp.full_like(m_scr, NEG_BIG)
    l_scr[...] = jnp.zeros_like(l_scr)
    acc_scr[...] = jnp.zeros_like(acc_scr)


def _prompt_attn_kernel(q_ref, k_ref, kt_ref, wuvt_ref, o_ref, acc_scr, *, heads, qb):
    i = pl.program_id(1)
    tk = qb
    hs = range(heads)
    k_idx = lax.broadcasted_iota(jnp.int32, (tk, qb), 0)
    q_idx = lax.broadcasted_iota(jnp.int32, (tk, qb), 1)
    causal = k_idx <= q_idx
    qs = [q_ref[h] for h in hs]

    def block(j, stats, masked):
        kj = k_ref[pl.ds(pl.multiple_of(j * tk, tk), tk), :]
        ktj = kt_ref[j]
        sts = [_mm_nt(kj, qs[h]) for h in hs]
        new = []
        for h in hs:
            st = jnp.where(causal, sts[h], NEG_BIG) if masked else sts[h]
            m_prev, l_prev = stats[h]
            m_new = jnp.maximum(m_prev, jnp.max(st, axis=0, keepdims=True))
            alpha = jnp.exp(m_prev - m_new)
            p = jnp.exp(st - m_new)
            l_new = alpha * l_prev + jnp.sum(p, axis=0, keepdims=True)
            acc_scr[h] = acc_scr[h] * alpha + jnp.dot(ktj, p.astype(bf16), preferred_element_type=f32)
            new.append((m_new, l_new))
        return tuple(new)

    acc_scr[...] = jnp.zeros_like(acc_scr)
    stats0 = tuple((jnp.full((1, qb), NEG_BIG, f32), jnp.zeros((1, qb), f32)) for _ in hs)
    stats = lax.fori_loop(0, i, lambda j, c: block(j, c, False), stats0)
    stats = block(i, stats, True)
    for h in hs:
        o_t = (acc_scr[h] / stats[h][1]).astype(bf16)
        head_t = jnp.dot(wuvt_ref[h], o_t, preferred_element_type=f32)
        o_ref[:, h * V_HEAD_DIM:(h + 1) * V_HEAD_DIM] = head_t.T.astype(o_ref.dtype)


def prompt_attention(q, keys, kt, w_uvt, batch, seq):
    heads = q.shape[0]
    qk = q.shape[2]
    qb = kt.shape[2]
    nq = seq // qb
    kern = functools.partial(_prompt_attn_kernel, heads=heads, qb=qb)
    return pl.pallas_call(
        kern,
        grid=(batch, nq),
        in_specs=[
            pl.BlockSpec((heads, qb, qk), lambda b, i: (0, b * nq + i, 0)),
            pl.BlockSpec((seq, qk), lambda b, i: (b, 0)),
            pl.BlockSpec((nq, KV_LORA, qb), lambda b, i: (b, 0, 0)),
            pl.BlockSpec((heads, V_HEAD_DIM, KV_LORA), lambda b, i: (0, 0, 0)),
        ],
        out_specs=pl.BlockSpec((qb, heads * V_HEAD_DIM), lambda b, i: (b * nq + i, 0)),
        out_shape=jax.ShapeDtypeStruct((batch * seq, heads * V_HEAD_DIM), bf16),
        scratch_shapes=[pltpu.VMEM((heads, KV_LORA, qb), f32)],
        compiler_params=_cparams("parallel", "arbitrary"),
        name="prompt_attention",
    )(q, keys, kt, w_uvt)


DECODE_PAGES_PER_CHUNK = 32
DECODE_SLOTS = 4
DECODE_SUBCHAINS = 2


def _decode_attn_kernel(pt_ref, q_ref, cnew_ref, penew_ref, wuv_ref, ck_hbm, pe_hbm, o_ref,
                        kbuf, pbuf, ksem, psem, m_scr, l_scr, acc_scr, *, heads, seq, n_chunks, n_rows, pages):
    b = pl.program_id(0)
    r = heads * seq
    ahead = DECODE_SLOTS - 1
    sub = pages // DECODE_SUBCHAINS
    q_all = q_ref[...].reshape(r, KV_LORA + ROPE_DIM).astype(bf16)
    q = q_all[:, :KV_LORA]
    qp = q_all[:, KV_LORA:]

    def chunk_copies(row, ch, slot):
        out = []
        for p in range(pages):
            page = pt_ref[row, ch * pages + p]
            out.append(pltpu.make_async_copy(ck_hbm.at[page], kbuf.at[slot, p], ksem.at[slot]))
            out.append(pltpu.make_async_copy(pe_hbm.at[page], pbuf.at[slot, p], psem.at[slot]))
        return out

    def start_chunk(row, ch):
        @pl.when(row < n_rows)
        def _():
            slot = (row * n_chunks + ch) & (DECODE_SLOTS - 1)
            for cp in chunk_copies(row, ch, slot):
                cp.start()

    def start_ahead(ch):
        wrap = ch + ahead >= n_chunks
        start_chunk(jnp.where(wrap, b + 1, b), jnp.where(wrap, ch + ahead - n_chunks, ch + ahead))

    @pl.when(b == 0)
    def _():
        for t in range(min(ahead, n_chunks * n_rows)):
            start_chunk(jnp.int32(t // n_chunks), jnp.int32(t % n_chunks))

    for g in range(DECODE_SUBCHAINS):
        _softmax_init(m_scr.at[g], l_scr.at[g], acc_scr.at[g])

    def consume(ch, carry):
        slot = (b * n_chunks + ch) & (DECODE_SLOTS - 1)
        for cp in chunk_copies(b, ch, slot):
            cp.wait()
        start_ahead(ch)
        for g in range(DECODE_SUBCHAINS):
            kc = kbuf[slot, g * sub:(g + 1) * sub].reshape(sub * PAGE_SIZE, KV_LORA).astype(bf16)
            kp_t = jnp.concatenate([pbuf[slot, p] for p in range(g * sub, (g + 1) * sub)], axis=1)
            s = _mm_nt(q, kc) + _mm(qp, kp_t)
            _softmax_update(s, lambda pm, kc=kc: jnp.dot(pm, kc, preferred_element_type=f32),
                            m_scr.at[g], l_scr.at[g], acc_scr.at[g])
        return carry

    lax.fori_loop(0, n_chunks, consume, 0)

    kc = jnp.concatenate([cnew_ref[0], jnp.zeros((PAGE_SIZE - seq, KV_LORA), f32)], axis=0).astype(bf16)
    kp = jnp.concatenate([penew_ref[0], jnp.zeros((PAGE_SIZE - seq, ROPE_DIM), f32)], axis=0)
    s = _mm_nt(q, kc) + _mm_nt(qp, kp)
    q_t = lax.rem(lax.broadcasted_iota(jnp.int32, (r, PAGE_SIZE), 0), jnp.int32(seq))
    k_t = lax.broadcasted_iota(jnp.int32, (r, PAGE_SIZE), 1)
    s = jnp.where(k_t <= q_t, s, NEG_BIG)
    _softmax_update(s, lambda pm: jnp.dot(pm, kc, preferred_element_type=f32), m_scr.at[0], l_scr.at[0], acc_scr.at[0])
    m_all = m_scr[0]
    for g in range(1, DECODE_SUBCHAINS):
        m_all = jnp.maximum(m_all, m_scr[g])
    l_all = jnp.zeros_like(m_all)
    acc_all = jnp.zeros((r, KV_LORA), f32)
    for g in range(DECODE_SUBCHAINS):
        w = jnp.exp(m_scr[g] - m_all)
        l_all = l_all + w * l_scr[g]
        acc_all = acc_all + w[:, 0:1] * acc_scr[g]
    o_lat = (acc_all / l_all[:, 0:1]).astype(bf16)
    for hh in range(heads):
        full = jnp.dot(o_lat, wuv_ref[hh], preferred_element_type=f32)
        o_ref[:, hh * V_HEAD_DIM:(hh + 1) * V_HEAD_DIM] = full[hh * seq:(hh + 1) * seq]


def decode_attention(q, c_new, pe_new, cache_ckv, cache_kpe_t, page_table, w_uv3):
    heads = q.shape[0]
    db, seq, _ = c_new.shape
    n_pages = page_table.shape[1]
    pages = DECODE_PAGES_PER_CHUNK
    assert n_pages % pages == 0 and pages % DECODE_SUBCHAINS == 0
    n_chunks = n_pages // pages
    assert n_chunks >= DECODE_SLOTS - 1
    r = heads * seq
    kern = functools.partial(_decode_attn_kernel, heads=heads, seq=seq, n_chunks=n_chunks, n_rows=db, pages=pages)
    grid_spec = pltpu.PrefetchScalarGridSpec(
        num_scalar_prefetch=1,
        grid=(db,),
        in_specs=[
            pl.BlockSpec((heads, seq, KV_LORA + ROPE_DIM), lambda b, pt: (0, b, 0)),
            pl.BlockSpec((1, seq, KV_LORA), lambda b, pt: (b, 0, 0)),
            pl.BlockSpec((1, seq, ROPE_DIM), lambda b, pt: (b, 0, 0)),
            pl.BlockSpec((heads, KV_LORA, V_HEAD_DIM), lambda b, pt: (0, 0, 0)),
            pl.BlockSpec(memory_space=pl.ANY),
            pl.BlockSpec(memory_space=pl.ANY),
        ],
        out_specs=pl.BlockSpec((seq, heads * V_HEAD_DIM), lambda b, pt: (b, 0)),
        scratch_shapes=[
            pltpu.VMEM((DECODE_SLOTS, pages, PAGE_SIZE, KV_LORA), f32),
            pltpu.VMEM((DECODE_SLOTS, pages, ROPE_DIM, PAGE_SIZE), f32),
            pltpu.SemaphoreType.DMA((DECODE_SLOTS,)),
            pltpu.SemaphoreType.DMA((DECODE_SLOTS,)),
            pltpu.VMEM((DECODE_SUBCHAINS, r, LANES), f32),
            pltpu.VMEM((DECODE_SUBCHAINS, r, LANES), f32),
            pltpu.VMEM((DECODE_SUBCHAINS, r, KV_LORA), f32),
        ],
    )
    return pl.pallas_call(
        kern,
        grid_spec=grid_spec,
        out_shape=jax.ShapeDtypeStruct((db * seq, heads * V_HEAD_DIM), f32),
        compiler_params=_cparams("arbitrary"),
        name="decode_attention",
    )(page_table, q, c_new, pe_new, w_uv3, cache_ckv, cache_kpe_t)


def _rope_tables(first_pos, seq):
    half = ROPE_DIM // 2
    inv_freq = ROPE_THETA ** (-jnp.arange(half, dtype=f32) / half)
    pos = first_pos + jnp.arange(seq, dtype=jnp.int32)
    ang = pos.astype(f32)[:, None] * inv_freq[None, :]
    cos, sin = jnp.cos(ang), jnp.sin(ang)
    return jnp.concatenate([cos, cos], axis=1), jnp.concatenate([-sin, sin], axis=1)


def _run_group(x3, gdn_s0, conv0, past, wts):
    batch, seq, d = x3.shape
    n_a = wts["gdn_w_in"].shape[0]
    depth = wts["ffn_in_b"].shape[0]
    x = x3.reshape(batch * seq, d)
    past_len = 0 if past is None else past[2].shape[1] * PAGE_SIZE
    cos2, sin2 = _rope_tables(past_len, seq)
    if seq < LANES:
        cos2, sin2 = jnp.tile(cos2, (batch, 1)), jnp.tile(sin2, (batch, 1))
    cos2h, sin2h = jnp.tile(cos2, (1, MLA_HEADS)), jnp.tile(sin2, (1, MLA_HEADS))
    act_dtype = bf16 if past is None else f32
    per_layer = wts["norm_pre"].shape[0] // depth
    pre = lambda layer, k: (wts["norm_pre"], layer * per_layer + k)
    post = lambda layer, k: (wts["norm_post"], layer * per_layer + k)
    states, convs = [], []
    c_new = pe_new = keys_bf = kt_bf = None
    attn_tk = _row_tile(seq, 256) if past is None else None
    for layer in range(depth):
        if layer == n_a:
            kv = mla_shared_kv(x, wts["mla_kvin_norm"], wts["mla_w_dkv"], wts["mla_kv_norm"], cos2, sin2, attn_tk)
            c_new, pe_new = kv[0], kv[1]
            if past is None:
                keys_bf, kt_bf = kv[2], kv[3]
        x = ffn_sublayer(x, pre(layer, 0), wts["ffn_in_b"], wts["ffn_out_b"], post(layer, 0), layer, 0)
        if layer < n_a:
            proj = norm_proj(x, pre(layer, 1), wts["gdn_w_in_b"], layer, act_dtype)
            proj3 = proj.reshape(batch, seq, -1)
            conv_dim = conv0.shape[-1]
            o, s_new = gdn_core(proj3, wts["gdn_conv_w"][layer], conv0[layer], gdn_s0[layer],
                                wts["gdn_dt_bias"][layer], wts["gdn_a_log"][layer], wts["gdn_norm"][layer], act_dtype)
            states.append(s_new)
            if seq >= CONV_W - 1:
                convs.append(proj3[:, seq - (CONV_W - 1):, :conv_dim].astype(conv0.dtype))
            else:
                prev = jnp.concatenate([conv0[layer].astype(f32), proj3[:, :, :conv_dim]], axis=1)
                convs.append(prev[:, -(CONV_W - 1):])
            mixer_out = (o.reshape(batch * seq, -1), wts["gdn_w_out_b"][layer], post(layer, 1))
        else:
            jb = layer - n_a
            q = mla_query(x, pre(layer, 1), wts["mla_w_dq"][jb], wts["mla_q_norm"][jb],
                                    wts["mla_w_uq"][jb], wts["mla_w_uk"], cos2h, sin2h, act_dtype)
            if past is None:
                o = prompt_attention(q, keys_bf, kt_bf, wts["w_uvt"], batch, seq)
            else:
                o = decode_attention(q, c_new.reshape(batch, seq, -1), pe_new.reshape(batch, seq, -1),
                                     past[0], past[1], past[2], wts["w_uv3"])
            mixer_out = (o, wts["mla_w_o_b"][jb], post(layer, 1))
        x = ffn_sublayer(x, pre(layer, 2), wts["ffn_in_b"], wts["ffn_out_b"], post(layer, 2), layer, 1, mixer_out)
    return (x.reshape(batch, seq, d), jnp.stack(states), jnp.stack(convs),
            c_new.reshape(batch, seq, -1), pe_new.reshape(batch, seq, -1))


def _pack_gdn_w_in(w_in, heads):
    main = 4 * heads * GDN_HEAD_DIM
    padw = ((0, 0), (0, 0), (0, LANES - heads))
    b = jnp.pad(w_in[..., main:main + heads], padw)
    a = jnp.pad(w_in[..., main + heads:main + 2 * heads], padw)
    return jnp.concatenate([w_in[..., :main], b, a], axis=-1).astype(bf16)


def kernel(x_prompt, x_sample, state_gdn, state_conv, cache_ckv, cache_kpe, page_table, norm_pre, norm_post, ffn_in, ffn_out, gdn_w_in, gdn_conv_w, gdn_dt_bias, gdn_a_log, gdn_norm, gdn_w_out, mla_kvin_norm, mla_w_dkv, mla_kv_norm, mla_w_uk, mla_w_uv, mla_w_dq, mla_q_norm, mla_w_uq, mla_w_o):
    n_a = gdn_w_in.shape[0]
    b = x_prompt.shape[0]
    heads = state_gdn.shape[2]
    wts = dict(
        norm_pre=norm_pre.reshape(-1, 1, norm_pre.shape[-1]), norm_post=norm_post.reshape(-1, 1, norm_post.shape[-1]),
        ffn_in_b=ffn_in.astype(bf16), ffn_out_b=ffn_out.astype(bf16),
        gdn_w_in=gdn_w_in, gdn_w_in_b=_pack_gdn_w_in(gdn_w_in, heads), gdn_conv_w=gdn_conv_w,
        gdn_dt_bias=gdn_dt_bias, gdn_a_log=gdn_a_log, gdn_norm=gdn_norm, gdn_w_out_b=gdn_w_out.astype(bf16),
        mla_kvin_norm=mla_kvin_norm, mla_w_dkv=mla_w_dkv, mla_kv_norm=mla_kv_norm, mla_w_uk=mla_w_uk,
        w_uv3=jnp.transpose(mla_w_uv.reshape(KV_LORA, MLA_HEADS, V_HEAD_DIM), (1, 0, 2)).astype(bf16),
        w_uvt=jnp.transpose(mla_w_uv.reshape(KV_LORA, MLA_HEADS, V_HEAD_DIM), (1, 2, 0)).astype(bf16),
        mla_w_dq=mla_w_dq, mla_q_norm=mla_q_norm, mla_w_uq=mla_w_uq, mla_w_o_b=mla_w_o.astype(bf16),
    )
    zero_s = jnp.zeros((n_a, b) + state_gdn.shape[2:], state_gdn.dtype)
    zero_conv = jnp.zeros((n_a, b) + state_conv.shape[2:], state_conv.dtype)
    y_p, sg_p, cv_p, ckv_p, kpe_p = _run_group(x_prompt, zero_s, zero_conv, None, wts)
    y_s, sg_s, cv_s, ckv_s, kpe_s = _run_group(x_sample, state_gdn, state_conv,
                                               (cache_ckv, jnp.swapaxes(cache_kpe, 1, 2), page_table), wts)
    return (y_p, y_s, sg_p, cv_p, ckv_p, kpe_p, sg_s, cv_s, ckv_s, kpe_s)
```
